```python
import math
import jax, jax.numpy as jnp
from jax import lax
import numpy as np

D_MODEL = 2048
BATCH = 8
SEQ = 2048
DEPTH = 2
DEC_BATCH = 128
DEC_SEQ = 4
PAST_LEN = 2048
PAGE_SIZE = 128

HEAD_DIM = 128
GDN_HEADS = 8
GDN_WIDTH = GDN_HEADS * HEAD_DIM
MOBA_HEADS = 8
MOBA_WIDTH = MOBA_HEADS * HEAD_DIM
MIX_WIDTH = GDN_WIDTH + MOBA_WIDTH
GDN_CONV = 4
GDN_CHUNK = 64
MOBA_BLOCK = 256
MOBA_TOPK = 3
MOBA_GATHER_ROWS = 32
D_FF = 5632
FFN_CONV = 3
NORM_EPS = 1e-6

O_Z = 3 * GDN_WIDTH
O_A = 4 * GDN_WIDTH
O_B = O_A + GDN_HEADS
O_MQ = O_B + GDN_HEADS
O_MK = O_MQ + MOBA_WIDTH
O_MV = O_MK + MOBA_WIDTH
N_IN = O_MV + MOBA_WIDTH

kernel_name = 'hymba_gdn_moba_convffn_step'


def _rmsnorm(x, g):
    xf = x.astype(jnp.float32)
    y = xf * lax.rsqrt(jnp.mean(xf * xf, axis=-1, keepdims=True) + NORM_EPS)
    return (y * g.astype(jnp.float32)).astype(x.dtype)


def _l2norm(x):
    return x * lax.rsqrt(jnp.sum(x * x, axis=-1, keepdims=True) + NORM_EPS)


def _causal_dwconv(x, buf, w):
    width = w.shape[0]
    xe = jnp.concatenate([buf.astype(x.dtype), x], axis=1)
    y = lax.conv_general_dilated(
        xe, w.astype(x.dtype)[:, None, :], window_strides=(1,), padding='VALID',
        dimension_numbers=('NWC', 'WIO', 'NWC'), feature_group_count=x.shape[-1])
    return y, xe[:, xe.shape[1] - (width - 1):]


def _gated_delta_rule(q, k, v, g, beta, s0):
    B, T, H, DK = k.shape
    DV = v.shape[-1]
    C = math.gcd(T, GDN_CHUNK)
    N = T // C

    def chunks(a):
        a = jnp.moveaxis(a, 2, 1)
        return a.reshape(a.shape[:2] + (N, C) + a.shape[3:])

    q, k, v, g, beta = (chunks(a) for a in (q, k, v, g, beta))
    G = jnp.cumsum(g, axis=-1)
    causal = jnp.tril(jnp.ones((C, C), bool))
    strict = jnp.tril(jnp.ones((C, C), bool), -1)
    decay = jnp.exp(jnp.where(causal, G[..., :, None] - G[..., None, :], -jnp.inf))
    kk = jnp.einsum('bhncd,bhnsd->bhncs', k, k)
    a_mat = jnp.where(strict, beta[..., None] * kk * decay, 0.0)
    rhs = jnp.concatenate([beta[..., None] * v, (beta * jnp.exp(G))[..., None] * k], axis=-1)
    sol = jax.lax.linalg.triangular_solve(a_mat, rhs, left_side=True, lower=True, unit_diagonal=True)
    u, w = sol[..., :DV], sol[..., DV:]
    qk = jnp.where(causal, jnp.einsum('bhncd,bhnsd->bhncs', q, k) * decay, 0.0)
    q_dec = q * jnp.exp(G)[..., None]
    k_dec = k * jnp.exp(G[..., -1:] - G)[..., None]
    g_last = jnp.exp(G[..., -1])

    def step(s, xs):
        u_c, w_c, qk_c, qd_c, kd_c, gl_c = xs
        v_new = u_c - jnp.einsum('bhcd,bhde->bhce', w_c, s)
        o = jnp.einsum('bhcd,bhde->bhce', qd_c, s) + jnp.einsum('bhcs,bhse->bhce', qk_c, v_new)
        s = s * gl_c[..., None, None] + jnp.einsum('bhcd,bhce->bhde', kd_c, v_new)
        return s, o

    xs = tuple(jnp.moveaxis(a, 2, 0) for a in (u, w, qk, q_dec, k_dec, g_last))
    s_final, o = lax.scan(step, s0, xs)
    o = jnp.moveaxis(o, 0, 2).reshape(B, H, T, DV)
    return jnp.moveaxis(o, 1, 2), s_final


def _moba_attention(q, k, v, q_pos):
    B, Q, H, D = q.shape
    L = k.shape[1]
    nb = -(-L // MOBA_BLOCK)
    pad = nb * MOBA_BLOCK - L

    def blocks(a):
        a = jnp.pad(a, ((0, 0), (0, pad), (0, 0), (0, 0)))
        return jnp.moveaxis(a.reshape(B, nb, MOBA_BLOCK, H, D), 3, 1)

    kb, vb = blocks(k), blocks(v)
    qh = jnp.moveaxis(q, 2, 1)
    k_mean = jnp.mean(kb.astype(jnp.float32), axis=3)
    own = q_pos // MOBA_BLOCK
    gate = jnp.einsum('bhqd,bhnd->bhqn', qh.astype(jnp.float32), k_mean)
    fully_past = jnp.arange(nb)[None, :] < own[:, None]
    gate = jnp.where(fully_past, gate, -jnp.inf)
    n_sel = min(MOBA_TOPK, nb)
    _, top = lax.top_k(gate, n_sel)
    own_b = jnp.broadcast_to(own[None, None, :, None], (B, H, Q, 1)).astype(top.dtype)
    idx = jnp.concatenate([top, own_b], axis=-1)
    slot_ok = jnp.concatenate([jnp.arange(n_sel)[None, :] < jnp.minimum(own, MOBA_TOPK)[:, None],
                               jnp.ones((Q, 1), bool)], axis=-1)
    n_slot = n_sel + 1

    cap = max(1, MOBA_GATHER_ROWS // B)
    qc = max(d for d in range(1, min(cap, Q) + 1) if Q % d == 0)
    nq = Q // qc
    bi = jnp.arange(B)[:, None, None, None]
    hi = jnp.arange(H)[None, :, None, None]
    scale = D ** -0.5

    def attend(xs):
        q_c, idx_c, ok_c, pos_c = xs
        kg = kb[bi, hi, idx_c]
        vg = vb[bi, hi, idx_c]
        key_pos = idx_c[..., None] * MOBA_BLOCK + jnp.arange(MOBA_BLOCK)
        mask = ok_c[:, :, None] & (key_pos <= pos_c[:, None, None])
        logits = jnp.einsum('bhqd,bhqskd->bhqsk', q_c, kg, preferred_element_type=jnp.float32) * scale
        logits = jnp.where(mask, logits, -jnp.inf)
        p = jax.nn.softmax(logits.reshape(B, H, qc, n_slot * MOBA_BLOCK), axis=-1).reshape(logits.shape)
        return jnp.einsum('bhqsk,bhqskd->bhqd', p.astype(vg.dtype), vg, preferred_element_type=jnp.float32)

    xs = (qh.reshape(B, H, nq, qc, D).transpose(2, 0, 1, 3, 4),
          idx.reshape(B, H, nq, qc, n_slot).transpose(2, 0, 1, 3, 4),
          slot_ok.reshape(nq, qc, n_slot),
          q_pos.reshape(nq, qc))
    o = lax.map(attend, xs)
    return o.transpose(1, 0, 3, 2, 4).reshape(B, Q, H, D)


def _mixer(h, s0, conv_buf, k_past, v_past, q_pos,
           w_in, w_conv, a_log, dt_bias, g_gdn_out, g_q, g_k, g_attn_out, w_out):
    B, T, _ = h.shape
    proj = h @ w_in
    qkv, z = proj[..., :O_Z], proj[..., O_Z:O_A]
    a, b = proj[..., O_A:O_B], proj[..., O_B:O_MQ]
    mq, mk, mv = proj[..., O_MQ:O_MK], proj[..., O_MK:O_MV], proj[..., O_MV:]

    qkv_c, new_conv = _causal_dwconv(qkv, conv_buf, w_conv)
    qkv_c = jax.nn.silu(qkv_c.astype(jnp.float32))
    gq, gk, gv = (t.reshape(B, T, GDN_HEADS, HEAD_DIM) for t in jnp.split(qkv_c, 3, axis=-1))
    gq = _l2norm(gq) * HEAD_DIM ** -0.5
    gk = _l2norm(gk)
    beta = jax.nn.sigmoid(b.astype(jnp.float32))
    g = -jnp.exp(a_log.astype(jnp.float32)) * jax.nn.softplus(a.astype(jnp.float32) + dt_bias.astype(jnp.float32))
    o_gdn, s_new = _gated_delta_rule(gq, gk, gv, g, beta, s0.astype(jnp.float32))
    o_gdn = _rmsnorm(o_gdn, g_gdn_out) * jax.nn.silu(z.astype(jnp.float32)).reshape(B, T, GDN_HEADS, HEAD_DIM)

    mq = _rmsnorm(mq.reshape(B, T, MOBA_HEADS, HEAD_DIM), g_q)
    mk = _rmsnorm(mk.reshape(B, T, MOBA_HEADS, HEAD_DIM), g_k)
    mv = mv.reshape(B, T, MOBA_HEADS, HEAD_DIM)
    k_all = jnp.concatenate([k_past.astype(mk.dtype), mk], axis=1)
    v_all = jnp.concatenate([v_past.astype(mv.dtype), mv], axis=1)
    o_att = _rmsnorm(_moba_attention(mq, k_all, v_all, q_pos), g_attn_out)

    mixed = jnp.concatenate([o_gdn.reshape(B, T, GDN_WIDTH), o_att.reshape(B, T, MOBA_WIDTH)], axis=-1)
    return mixed.astype(h.dtype) @ w_out, s_new.astype(s0.dtype), new_conv, mk, mv


def _conv_ffn(h, buf, w_gate, w_up, w_dw, b_dw, w_down):
    gate_pre = h @ w_gate
    gc, new_buf = _causal_dwconv(gate_pre, buf, w_dw)
    act = jax.nn.silu(gc + b_dw) * (h @ w_up)
    return act @ w_down, new_buf


def _trunk(x, c, q_pos, past_rows, gdn_state, gdn_conv, ffn_conv,
           w_ada, b_ada, g_norm_mix, w_in, w_gdn_conv, gdn_a_log, gdn_dt_bias,
           g_gdn_out, g_q, g_k, g_attn_out, w_out, g_norm_ffn, w_ffn_gate, w_ffn_up,
           w_ffn_dw, b_ffn_dw, w_ffn_down):
    B, T, D = x.shape
    new_k, new_v, new_s, new_gc, new_fc = [], [], [], [], []
    for l in range(DEPTH):
        mod = (jax.nn.silu(c) @ w_ada[l] + b_ada[l]).reshape(B, 6, 1, D)
        shift1, scale1, gate1, shift2, scale2, gate2 = (mod[:, i] for i in range(6))
        h = _rmsnorm(x, g_norm_mix[l]) * (1 + scale1) + shift1
        k_past, v_past = past_rows(l)
        mix, s_l, gc_l, k_l, v_l = _mixer(
            h, gdn_state[l], gdn_conv[l], k_past, v_past, q_pos,
            w_in[l], w_gdn_conv[l], gdn_a_log[l], gdn_dt_bias[l], g_gdn_out[l],
            g_q[l], g_k[l], g_attn_out[l], w_out[l])
        x = x + gate1 * mix
        h = _rmsnorm(x, g_norm_ffn[l]) * (1 + scale2) + shift2
        f, fc_l = _conv_ffn(h, ffn_conv[l], w_ffn_gate[l], w_ffn_up[l], w_ffn_dw[l], b_ffn_dw[l], w_ffn_down[l])
        x = x + gate2 * f
        new_k.append(k_l)
        new_v.append(v_l)
        new_s.append(s_l)
        new_gc.append(gc_l)
        new_fc.append(fc_l)
    return (x, jnp.stack(new_k), jnp.stack(new_v), jnp.stack(new_s), jnp.stack(new_gc), jnp.stack(new_fc))


def setup_inputs(seed: int = 0) -> dict:
    key = jax.random.key(seed)
    ks = list(jax.random.split(key, 32))

    def nrm(i, shape, s):
        return jax.random.normal(ks[i], shape, jnp.float32) * s

    def gain(i, shape):
        return 1.0 + nrm(i, shape, 0.02)

    n_pages = PAST_LEN // PAGE_SIZE
    n_pool = (5 * DEC_BATCH * n_pages) // 4
    page_table = jax.random.permutation(ks[0], n_pool)[:DEC_BATCH * n_pages].reshape(DEC_BATCH, n_pages).astype(jnp.int32)
    dt = jnp.exp(jax.random.uniform(ks[1], (DEPTH, GDN_HEADS), jnp.float32, math.log(1e-3), math.log(1e-1)))
    return {
        'x_prompt': nrm(2, (BATCH, SEQ, D_MODEL), 1.0),
        'x_sample': nrm(3, (DEC_BATCH, DEC_SEQ, D_MODEL), 1.0),
        'cache_k': nrm(4, (DEPTH, n_pool, PAGE_SIZE, MOBA_HEADS, HEAD_DIM), 1.0),
        'cache_v': nrm(5, (DEPTH, n_pool, PAGE_SIZE, MOBA_HEADS, HEAD_DIM), 1.0),
        'state_gdn': nrm(6, (DEPTH, DEC_BATCH, GDN_HEADS, HEAD_DIM, HEAD_DIM), 0.1),
        'state_gdn_conv': nrm(7, (DEPTH, DEC_BATCH, GDN_CONV - 1, 3 * GDN_WIDTH), 1.0),
        'state_ffn_conv': nrm(8, (DEPTH, DEC_BATCH, FFN_CONV - 1, D_FF), 1.0),
        'page_table': page_table,
        'c_prompt': nrm(9, (BATCH, D_MODEL), 1.0),
        'c_sample': nrm(10, (DEC_BATCH, D_MODEL), 1.0),
        'w_ada': nrm(11, (DEPTH, D_MODEL, 6 * D_MODEL), 0.5 * D_MODEL ** -0.5),
        'b_ada': nrm(12, (DEPTH, 6 * D_MODEL), 0.02),
        'g_norm_mix': gain(13, (DEPTH, D_MODEL)),
        'w_in': nrm(14, (DEPTH, D_MODEL, N_IN), D_MODEL ** -0.5),
        'w_gdn_conv': nrm(15, (DEPTH, GDN_CONV, 3 * GDN_WIDTH), GDN_CONV ** -0.5),
        'gdn_a_log': jnp.log(jax.random.uniform(ks[16], (DEPTH, GDN_HEADS), jnp.float32, 1.0, 16.0)),
        'gdn_dt_bias': dt + jnp.log(-jnp.expm1(-dt)),
        'g_gdn_out': gain(17, (DEPTH, HEAD_DIM)),
        'g_q': gain(18, (DEPTH, HEAD_DIM)),
        'g_k': gain(19, (DEPTH, HEAD_DIM)),
        'g_attn_out': gain(20, (DEPTH, HEAD_DIM)),
        'w_out': nrm(21, (DEPTH, MIX_WIDTH, D_MODEL), MIX_WIDTH ** -0.5),
        'g_norm_ffn': gain(22, (DEPTH, D_MODEL)),
        'w_ffn_gate': nrm(23, (DEPTH, D_MODEL, D_FF), D_MODEL ** -0.5),
        'w_ffn_up': nrm(24, (DEPTH, D_MODEL, D_FF), D_MODEL ** -0.5),
        'w_ffn_dw': nrm(25, (DEPTH, FFN_CONV, D_FF), FFN_CONV ** -0.5),
        'b_ffn_dw': nrm(26, (DEPTH, D_FF), 0.02),
        'w_ffn_down': nrm(27, (DEPTH, D_FF, D_MODEL), D_FF ** -0.5),
    }


def reference(x_prompt, x_sample, cache_k, cache_v, state_gdn, state_gdn_conv, state_ffn_conv, page_table,
              c_prompt, c_sample, w_ada, b_ada, g_norm_mix, w_in, w_gdn_conv, gdn_a_log, gdn_dt_bias,
              g_gdn_out, g_q, g_k, g_attn_out, w_out, g_norm_ffn, w_ffn_gate, w_ffn_up, w_ffn_dw,
              b_ffn_dw, w_ffn_down):
    weights = (w_ada, b_ada, g_norm_mix, w_in, w_gdn_conv, gdn_a_log, gdn_dt_bias, g_gdn_out, g_q, g_k,
               g_attn_out, w_out, g_norm_ffn, w_ffn_gate, w_ffn_up, w_ffn_dw, b_ffn_dw, w_ffn_down)

    B, T, _ = x_prompt.shape
    dt = x_prompt.dtype
    empty = jnp.zeros((B, 0, MOBA_HEADS, HEAD_DIM), dt)

    def no_past(l):
        return empty, empty

    (y_prompt, k_prompt, v_prompt, gdn_prompt, gdn_conv_prompt, ffn_conv_prompt) = _trunk(
        x_prompt, c_prompt, jnp.arange(T, dtype=jnp.int32), no_past,
        jnp.zeros((DEPTH, B, GDN_HEADS, HEAD_DIM, HEAD_DIM), state_gdn.dtype),
        jnp.zeros((DEPTH, B, GDN_CONV - 1, 3 * GDN_WIDTH), dt),
        jnp.zeros((DEPTH, B, FFN_CONV - 1, D_FF), dt), *weights)

    DB, S, _ = x_sample.shape
    past_len = page_table.shape[1] * cache_k.shape[2]

    def paged_past(l):
        kp = cache_k[l][page_table].reshape(DB, past_len, MOBA_HEADS, HEAD_DIM)
        vp = cache_v[l][page_table].reshape(DB, past_len, MOBA_HEADS, HEAD_DIM)
        return kp, vp

    (y_sample, k_sample, v_sample, gdn_sample, gdn_conv_sample, ffn_conv_sample) = _trunk(
        x_sample, c_sample, past_len + jnp.arange(S, dtype=jnp.int32), paged_past,
        state_gdn, state_gdn_conv, state_ffn_conv, *weights)

    return (y_prompt, y_sample, k_prompt, v_prompt, gdn_prompt, gdn_conv_prompt, ffn_conv_prompt,
            k_sample, v_sample, gdn_sample, gdn_conv_sample, ffn_conv_sample)
```

```python
import functools

import jax
import jax.numpy as jnp
from jax import lax
from jax.experimental import pallas as pl
from jax.experimental.pallas import tpu as pltpu

F32 = jnp.float32
BF16 = jnp.bfloat16

NORM_EPS = 1e-6
HEAD_DIM = 128
GDN_CONV = 4
GDN_CHUNK = 64
MOBA_BLOCK = 256
MOBA_TOPK = 3
FFN_CONV = 3
SUBLANES = 8
VMEM_LIMIT = 56 * 1024 * 1024

_NT = (((1,), (1,)), ((), ()))
_NN = (((1,), (0,)), ((), ()))
_TN = (((0,), (0,)), ((), ()))


def _params(sem):
    return pltpu.CompilerParams(dimension_semantics=sem, vmem_limit_bytes=VMEM_LIMIT)


def _sigmoid(x):
    return 1.0 / (1.0 + jnp.exp(-x))


def _silu(x):
    return x * _sigmoid(x)


def _dot(a, b, dims=_NN):
    return lax.dot_general(a.astype(BF16), b.astype(BF16), dims, preferred_element_type=F32)


def _split(a):
    hi = a.astype(BF16)
    lo = (a - hi.astype(F32)).astype(BF16)
    return hi, lo


def _dot3(a, b, dims=_NN):
    ah, al = _split(a)
    bh, bl = _split(b)
    d = lambda x, y: lax.dot_general(x, y, dims, preferred_element_type=F32)
    return d(ah, bh) + (d(ah, bl) + d(al, bh))


def _norm_mod(x, g, scale, shift):
    ms = jnp.mean(x * x, axis=-1, keepdims=True)
    y = x * lax.rsqrt(ms + NORM_EPS) * g
    return y * (1.0 + scale) + shift


def _head_rmsnorm(x, g):
    ms = jnp.mean(x * x, axis=-1, keepdims=True)
    return x * lax.rsqrt(ms + NORM_EPS) * g


def _ada_kernel(c_ref, w_ref, b_ref, o_ref):
    s = _silu(c_ref[...])
    o_ref[...] = _dot(s, w_ref[...]) + b_ref[...]


def _ada(c, w, b, tn=1024):
    r, d = c.shape
    n = w.shape[1]
    return pl.pallas_call(
        _ada_kernel,
        grid=(n // tn,),
        in_specs=[pl.BlockSpec((r, d), lambda j: (0, 0)),
                  pl.BlockSpec((d, tn), lambda j: (0, j)),
                  pl.BlockSpec((1, tn), lambda j: (0, j))],
        out_specs=pl.BlockSpec((r, tn), lambda j: (0, j)),
        out_shape=jax.ShapeDtypeStruct((r, n), F32),
        compiler_params=_params(("arbitrary",)),
        name="ada",
    )(c, w, b.reshape(1, n))


def _mod_spec(per_row, tm, width, rows_per_seq, col_of=None):
    if col_of is None:
        col_of = lambda j: 0
    if per_row:
        return pl.BlockSpec((tm, width), lambda i, j: (i, col_of(j)))
    tps = rows_per_seq // tm
    return pl.BlockSpec((1, 1, width), lambda i, j: (i // tps, 0, col_of(j)))


def _mod_val(ref, per_row):
    return ref[...] if per_row else ref[0]


def _inproj_gdn_kernel(x_ref, sc_ref, sh_ref, g_ref, w_ref, wab_ref, o_ref, oab_ref, h_scr, *, per_row):
    @pl.when(pl.program_id(1) == 0)
    def _():
        h = _norm_mod(x_ref[...], g_ref[...], _mod_val(sc_ref, per_row), _mod_val(sh_ref, per_row))
        hb = h.astype(BF16)
        h_scr[...] = hb
        oab_ref[...] = jnp.dot(hb, wab_ref[...], preferred_element_type=F32)

    o_ref[...] = jnp.dot(h_scr[...], w_ref[...], preferred_element_type=F32)


def _inproj_gdn(x, scale, shift, g, w, wab, *, per_row, rows_per_seq, tm, tn):
    m, d = x.shape
    n = w.shape[1]
    nab = wab.shape[1]
    mod = _mod_spec(per_row, tm, d, rows_per_seq)
    return pl.pallas_call(
        functools.partial(_inproj_gdn_kernel, per_row=per_row),
        grid=(m // tm, n // tn),
        in_specs=[pl.BlockSpec((tm, d), lambda i, j: (i, 0)), mod, mod,
                  pl.BlockSpec((1, d), lambda i, j: (0, 0)),
                  pl.BlockSpec((d, tn), lambda i, j: (0, j)),
                  pl.BlockSpec((d, nab), lambda i, j: (0, 0))],
        out_specs=[pl.BlockSpec((tm, tn), lambda i, j: (i, j)),
                   pl.BlockSpec((tm, nab), lambda i, j: (i, 0))],
        out_shape=[jax.ShapeDtypeStruct((m, n), F32), jax.ShapeDtypeStruct((m, nab), F32)],
        scratch_shapes=[pltpu.VMEM((tm, d), BF16)],
        compiler_params=_params(("arbitrary", "arbitrary")),
        name="inproj_gdn",
    )(x, scale, shift, g, w, wab)


def _inproj_moba_kernel(x_ref, sc_ref, sh_ref, g_ref, w_ref, gains_ref, o_ref, h_scr, *, per_row, tiles_per_sec):
    j = pl.program_id(1)

    @pl.when(j == 0)
    def _():
        h = _norm_mod(x_ref[...], g_ref[...], _mod_val(sc_ref, per_row), _mod_val(sh_ref, per_row))
        h_scr[...] = h.astype(BF16)

    r = jnp.dot(h_scr[...], w_ref[...], preferred_element_type=F32)
    tn = r.shape[1]

    @pl.when(j < 2 * tiles_per_sec)
    def _():
        gain = gains_ref[0]
        for c in range(tn // HEAD_DIM):
            cols = slice(c * HEAD_DIM, (c + 1) * HEAD_DIM)
            o_ref[0, :, cols] = _head_rmsnorm(r[:, cols], gain)

    @pl.when(j >= 2 * tiles_per_sec)
    def _():
        o_ref[0] = r


def _inproj_moba(x, scale, shift, g, w, gains, *, per_row, rows_per_seq, tm, tn):
    m, d = x.shape
    width = w.shape[1] // 3
    tps = width // tn
    mod = _mod_spec(per_row, tm, d, rows_per_seq)
    return pl.pallas_call(
        functools.partial(_inproj_moba_kernel, per_row=per_row, tiles_per_sec=tps),
        grid=(m // tm, 3 * tps),
        in_specs=[pl.BlockSpec((tm, d), lambda i, j: (i, 0)), mod, mod,
                  pl.BlockSpec((1, d), lambda i, j: (0, 0)),
                  pl.BlockSpec((d, tn), lambda i, j: (0, j)),
                  pl.BlockSpec((1, 1, HEAD_DIM), lambda i, j: (jnp.minimum(j // tps, 1), 0, 0))],
        out_specs=pl.BlockSpec((1, tm, tn), lambda i, j: (j // tps, i, j % tps)),
        out_shape=jax.ShapeDtypeStruct((3, m, width), F32),
        scratch_shapes=[pltpu.VMEM((tm, d), BF16)],
        compiler_params=_params(("arbitrary", "arbitrary")),
        name="inproj_moba",
    )(x, scale, shift, g, w, gains)


def _gdn_kernel(qkv_ref, z_ref, ab_ref, cst_ref, s0_ref, wconv_ref, alog_ref, dtb_ref, gout_ref,
                o_ref, sfin_ref, cout_ref,
                s_scr, xe_scr, y_scr, g_scr, beta_scr, *, chunk, t_valid, nheads):
    t = pl.program_id(1)
    nt = pl.num_programs(1)
    tc = qkv_ref.shape[0]
    width = nheads * HEAD_DIM
    halo = SUBLANES

    @pl.when(t == 0)
    def _():
        s_scr[...] = s0_ref[0]
        xe_scr[0:halo] = cst_ref[0]

    x = qkv_ref[...]
    xe_scr[halo:halo + tc] = x
    w = wconv_ref[...]
    y = w[GDN_CONV - 1:GDN_CONV] * x
    for tap in range(GDN_CONV - 1):
        off = halo - (GDN_CONV - 1) + tap
        y = y + w[tap:tap + 1] * xe_scr[off:off + tc]
    y_scr[...] = _silu(y)

    ab = ab_ref[...]
    beta = _sigmoid(ab)
    sp_in = ab + dtb_ref[...]
    softplus = jnp.maximum(sp_in, 0.0) + jnp.log(1.0 + jnp.exp(-jnp.abs(sp_in)))
    g = -jnp.exp(alog_ref[...]) * softplus
    row = lax.broadcasted_iota(jnp.int32, (tc, HEAD_DIM), 0)
    if t_valid % tc != 0:
        valid = (t * tc + row) < t_valid
        beta = jnp.where(valid, beta, 0.0)
        g = jnp.where(valid, g, 0.0)
    in_chunk = row % chunk
    step = 1
    while step < chunk:
        g = g + jnp.where(in_chunk >= step, pltpu.roll(g, step, axis=0), 0.0)
        step *= 2
    g_scr[...] = g
    beta_scr[...] = beta

    ri = lax.broadcasted_iota(jnp.int32, (chunk, chunk), 0)
    ci = lax.broadcasted_iota(jnp.int32, (chunk, chunk), 1)
    causal = ri >= ci
    strict = ri > ci
    eye = jnp.where(ri == ci, 1.0, 0.0).astype(F32)
    gout = gout_ref[...]
    n_square = max(chunk.bit_length() - 2, 0)

    def chunk_body(c, carry):
        rows = pl.ds(pl.multiple_of(c * chunk, chunk), chunk)
        gc = g_scr[rows, :]
        gct = gc.T
        bc = beta_scr[rows, :]
        for h in range(nheads):
            q = y_scr[rows, h * HEAD_DIM:(h + 1) * HEAD_DIM]
            k = y_scr[rows, width + h * HEAD_DIM:width + (h + 1) * HEAD_DIM]
            v = y_scr[rows, 2 * width + h * HEAD_DIM:2 * width + (h + 1) * HEAD_DIM]
            q = q * lax.rsqrt(jnp.sum(q * q, axis=-1, keepdims=True) + NORM_EPS) * (HEAD_DIM ** -0.5)
            k = k * lax.rsqrt(jnp.sum(k * k, axis=-1, keepdims=True) + NORM_EPS)
            g_col = gc[:, h:h + 1]
            g_row = gct[h:h + 1, :]
            b_col = bc[:, nheads + h:nheads + h + 1]
            decay = jnp.where(causal, jnp.exp(g_col - g_row), 0.0)
            kk = _dot3(k, k, _NT)
            x_mat = jnp.where(strict, -(b_col * kk * decay), 0.0)
            inv = eye + x_mat
            pw = x_mat
            for _ in range(n_square):
                pw = _dot3(pw, pw)
                inv = inv + _dot3(inv, pw)
            eg = jnp.exp(g_col)
            rhs = jnp.concatenate([b_col * v, (b_col * eg) * k], axis=-1)
            sol = _dot3(inv, rhs)
            u = sol[:, :HEAD_DIM]
            wk = sol[:, HEAD_DIM:]
            qk = _dot(q, k, _NT) * decay
            g_end = gc[chunk - 1:chunk, h:h + 1]
            q_dec = q * eg
            k_dec = k * jnp.exp(g_end - g_col)
            s = s_scr[h]
            ws = _dot(jnp.concatenate([wk, q_dec], axis=0), s)
            v_new = u - ws[:chunk]
            o = ws[chunk:] + _dot(qk, v_new)
            s_scr[h] = s * jnp.exp(g_end) + _dot(k_dec, v_new, _TN)
            zh = z_ref[rows, h * HEAD_DIM:(h + 1) * HEAD_DIM]
            o_ref[rows, h * HEAD_DIM:(h + 1) * HEAD_DIM] = (_head_rmsnorm(o, gout) * _silu(zh)).astype(o_ref.dtype)
        return carry

    lax.fori_loop(0, tc // chunk, chunk_body, 0)

    xe_scr[0:halo] = xe_scr[tc:tc + halo]

    @pl.when(t == nt - 1)
    def _():
        sfin_ref[0] = s_scr[...]
        end = halo + t_valid - (t_valid - 1) // tc * tc
        cout_ref[0] = xe_scr[end - (GDN_CONV - 1):end]


def _gdn(qkvz, ab, conv_state8, s0, wconv, alog, dtb, gout, *, batch, seq, t_valid, tc, chunk):
    nheads = s0.shape[1]
    width = nheads * HEAD_DIM
    nt = seq // tc
    assert t_valid >= GDN_CONV - 1 and tc >= SUBLANES and (t_valid - 1) // tc == nt - 1
    return pl.pallas_call(
        functools.partial(_gdn_kernel, chunk=chunk, t_valid=t_valid, nheads=nheads),
        grid=(batch, nt),
        in_specs=[pl.BlockSpec((tc, 3 * width), lambda b, t: (b * nt + t, 0)),
                  pl.BlockSpec((tc, width), lambda b, t: (b * nt + t, 3)),
                  pl.BlockSpec((tc, HEAD_DIM), lambda b, t: (b * nt + t, 0)),
                  pl.BlockSpec((1, SUBLANES, 3 * width), lambda b, t: (b, 0, 0)),
                  pl.BlockSpec((1, nheads, HEAD_DIM, HEAD_DIM), lambda b, t: (b, 0, 0, 0)),
                  pl.BlockSpec((GDN_CONV, 3 * width), lambda b, t: (0, 0)),
                  pl.BlockSpec((1, HEAD_DIM), lambda b, t: (0, 0)),
                  pl.BlockSpec((1, HEAD_DIM), lambda b, t: (0, 0)),
                  pl.BlockSpec((1, HEAD_DIM), lambda b, t: (0, 0))],
        out_specs=[pl.BlockSpec((tc, width), lambda b, t: (b * nt + t, 0)),
                   pl.BlockSpec((1, nheads, HEAD_DIM, HEAD_DIM), lambda b, t: (b, 0, 0, 0)),
                   pl.BlockSpec((1, GDN_CONV - 1, 3 * width), lambda b, t: (b, 0, 0))],
        out_shape=[jax.ShapeDtypeStruct((batch * seq, width), BF16),
                   jax.ShapeDtypeStruct(s0.shape, F32),
                   jax.ShapeDtypeStruct((batch, GDN_CONV - 1, 3 * width), F32)],
        scratch_shapes=[pltpu.VMEM((nheads, HEAD_DIM, HEAD_DIM), F32),
                        pltpu.VMEM((tc + SUBLANES, 3 * width), F32),
                        pltpu.VMEM((tc, 3 * width), F32),
                        pltpu.VMEM((tc, HEAD_DIM), F32),
                        pltpu.VMEM((tc, HEAD_DIM), F32)],
        compiler_params=_params(("arbitrary", "arbitrary")),
        name="gdn",
    )(qkvz, qkvz, ab, conv_state8, s0, wconv, alog, dtb, gout)


def _select_blocks(gate, n_past, axis):
    nb = gate.shape[axis]
    idx = lax.broadcasted_iota(jnp.int32, gate.shape, axis)
    rank = jnp.zeros(gate.shape, F32)
    for m in range(nb):
        gm = gate[m:m + 1, :] if axis == 0 else gate[:, m:m + 1]
        beats = (gm > gate) | ((gm == gate) & (m < idx))
        rank = rank + jnp.where(beats & (m < n_past), 1.0, 0.0)
    return jnp.where((idx < n_past) & (rank < MOBA_TOPK), 1.0, 0.0)


def _moba_prompt_kernel(q_ref, k_ref, v_ref, gout_ref, o_ref, km_scr, vt_scr, sel_scr):
    j = pl.program_id(2)
    blk = MOBA_BLOCK
    nb = k_ref.shape[0] // blk
    scale = HEAD_DIM ** -0.5

    @pl.when(j == 0)
    def _():
        for n in range(nb):
            rows = slice(n * blk, (n + 1) * blk)
            km_scr[n:n + 1, :] = jnp.mean(k_ref[rows, :], axis=0, keepdims=True)
            vt_scr[:, rows] = v_ref[rows, :].T.astype(BF16)

    q = q_ref[...]
    gate_t = _dot3(km_scr[...], q, _NT)
    sel_scr[...] = _select_blocks(gate_t, j, axis=0)
    qb = q.astype(BF16)

    def scores(n):
        rows = pl.ds(pl.multiple_of(n * blk, blk), blk)
        return lax.dot_general(k_ref[rows, :].astype(BF16), qb, _NT, preferred_element_type=F32) * scale

    def values_t(n):
        return vt_scr[:, pl.ds(pl.multiple_of(n * blk, blk), blk)]

    ki = lax.broadcasted_iota(jnp.int32, (blk, blk), 0)
    qi = lax.broadcasted_iota(jnp.int32, (blk, blk), 1)
    s = jnp.where(ki <= qi, scores(j), -jnp.inf)
    m0 = jnp.max(s, axis=0, keepdims=True)
    p = jnp.exp(s - m0)
    l0 = jnp.sum(p, axis=0, keepdims=True)
    acc0 = jnp.dot(values_t(j), p.astype(BF16), preferred_element_type=F32)

    def body(n, carry):
        m, l, acc = carry
        keep = sel_scr[pl.ds(n, 1), :] > 0.0
        s = jnp.where(keep, scores(n), -jnp.inf)
        m_new = jnp.maximum(m, jnp.max(s, axis=0, keepdims=True))
        alpha = jnp.exp(m - m_new)
        p = jnp.exp(s - m_new)
        l = alpha * l + jnp.sum(p, axis=0, keepdims=True)
        acc = alpha * acc + jnp.dot(values_t(n), p.astype(BF16), preferred_element_type=F32)
        return m_new, l, acc

    _, l, acc = lax.fori_loop(0, j, body, (m0, l0, acc0))
    o = (acc / l).T
    o_ref[...] = _head_rmsnorm(o, gout_ref[...]).astype(o_ref.dtype)


def _moba_prompt(q, k, v, gout, *, batch, seq):
    width = q.shape[1]
    nheads = width // HEAD_DIM
    nb = seq // MOBA_BLOCK
    assert seq % MOBA_BLOCK == 0
    return pl.pallas_call(
        _moba_prompt_kernel,
        grid=(batch, nheads, nb),
        in_specs=[pl.BlockSpec((MOBA_BLOCK, HEAD_DIM), lambda b, h, j: (b * nb + j, h)),
                  pl.BlockSpec((seq, HEAD_DIM), lambda b, h, j: (b, h)),
                  pl.BlockSpec((seq, HEAD_DIM), lambda b, h, j: (b, h)),
                  pl.BlockSpec((1, HEAD_DIM), lambda b, h, j: (0, 0))],
        out_specs=pl.BlockSpec((MOBA_BLOCK, HEAD_DIM), lambda b, h, j: (b * nb + j, h)),
        out_shape=jax.ShapeDtypeStruct((batch * seq, width), BF16),
        scratch_shapes=[pltpu.VMEM((max(nb, SUBLANES), HEAD_DIM), F32),
                        pltpu.VMEM((HEAD_DIM, seq), BF16),
                        pltpu.VMEM((max(nb, SUBLANES), MOBA_BLOCK), F32)],
        compiler_params=_params(("arbitrary", "arbitrary", "arbitrary")),
        name="moba_prompt",
    )(q, k, v, gout)


def _moba_sample_kernel(pt_ref, q_ref, kn_ref, vn_ref, ka_ref, kb_ref, va_ref, vb_ref, gout_ref, o_ref,
                        qbd_scr, m_scr, l_scr, gate_scr, acc_scr, *, nheads, nq):
    del pt_ref
    n = pl.program_id(1)
    nb = pl.num_programs(1)
    width = nheads * HEAD_DIM
    nr = nq * nheads
    scale = HEAD_DIM ** -0.5
    lane_head = lax.broadcasted_iota(jnp.int32, (nr, width), 1) // HEAD_DIM
    row_head = lax.broadcasted_iota(jnp.int32, (nr, width), 0) % nheads
    diag = lane_head == row_head

    @pl.when(n == 0)
    def _():
        q = q_ref[0]
        rep = jnp.concatenate([jnp.broadcast_to(q[i:i + 1, :], (nheads, width)) for i in range(nq)], axis=0)
        qbd_scr[...] = jnp.where(diag, rep, 0.0)

    qbd = qbd_scr[...]
    k = jnp.concatenate([ka_ref[0], kb_ref[0]], axis=0)
    v = jnp.concatenate([va_ref[0], vb_ref[0]], axis=0)
    k_mean = jnp.mean(k, axis=0, keepdims=True)
    gate = jnp.sum(qbd * k_mean, axis=-1, keepdims=True)
    s = _dot(qbd, k, _NT) * scale
    m = jnp.max(s, axis=-1, keepdims=True)
    p = jnp.exp(s - m)
    l = jnp.sum(p, axis=-1, keepdims=True)
    pv = _dot(p, v)
    lane = lax.broadcasted_iota(jnp.int32, (nr, HEAD_DIM), 1)
    here = lane == n
    first = n == 0
    m_scr[...] = jnp.where(here, m, jnp.where(first, 0.0, m_scr[...]))
    l_scr[...] = jnp.where(here, l, jnp.where(first, 0.0, l_scr[...]))
    gate_scr[...] = jnp.where(here, gate, jnp.where(first, 0.0, gate_scr[...]))
    acc_scr[n] = pv

    @pl.when(n == nb - 1)
    def _():
        n_past = acc_scr.shape[0]
        sel = _select_blocks(gate_scr[:, 0:n_past], n_past, axis=1) > 0.0
        kn = kn_ref[0]
        vn = vn_ref[0]
        nq_pad = SUBLANES * ((nq + SUBLANES - 1) // SUBLANES)
        kn_p = jnp.concatenate([kn, jnp.zeros((nq_pad - nq, width), F32)], axis=0) if nq_pad > nq else kn
        vn_p = jnp.concatenate([vn, jnp.zeros((nq_pad - nq, width), F32)], axis=0) if nq_pad > nq else vn
        s_own = _dot(qbd, kn_p, _NT) * scale
        key_i = lax.broadcasted_iota(jnp.int32, (nr, nq_pad), 1)
        qry_i = lax.broadcasted_iota(jnp.int32, (nr, nq_pad), 0) // nheads
        s_own = jnp.where(key_i <= qry_i, s_own, -jnp.inf)
        m_own = jnp.max(s_own, axis=-1, keepdims=True)
        m_all = jnp.where(sel, m_scr[:, 0:n_past], -jnp.inf)
        m_fin = jnp.maximum(m_own, jnp.max(m_all, axis=-1, keepdims=True))
        p_own = jnp.exp(s_own - m_fin)
        wgt = jnp.where(sel, jnp.exp(m_all - m_fin), 0.0)
        l_fin = (jnp.sum(p_own, axis=-1, keepdims=True)
                 + jnp.sum(wgt * l_scr[:, 0:n_past], axis=-1, keepdims=True))
        o = _dot(p_own, vn_p)
        for b in range(n_past):
            o = o + wgt[:, b:b + 1] * acc_scr[b]
        o = jnp.where(diag, o / l_fin, 0.0)
        gout = gout_ref[...]
        for i in range(nq):
            oi = jnp.sum(o[i * nheads:(i + 1) * nheads, :], axis=0, keepdims=True)
            for h in range(nheads):
                cols = slice(h * HEAD_DIM, (h + 1) * HEAD_DIM)
                o_ref[0, i:i + 1, cols] = _head_rmsnorm(oi[:, cols], gout).astype(o_ref.dtype)


def _moba_sample(q, k_new, v_new, cache_k, cache_v, page_table, gout):
    db, nq, width = q.shape
    nheads = width // HEAD_DIM
    page = cache_k.shape[1]
    n_pages = page_table.shape[1]
    assert MOBA_BLOCK == 2 * page and n_pages % 2 == 0
    nb = n_pages // 2
    nr = nq * nheads
    assert nb <= HEAD_DIM and nr % SUBLANES == 0
    pt = page_table.reshape(-1)

    def page_spec(half):
        return pl.BlockSpec((1, page, width), lambda b, n, pt_ref: (pt_ref[b * n_pages + 2 * n + half], 0, 0))

    tok = pl.BlockSpec((1, nq, width), lambda b, n, pt_ref: (b, 0, 0))
    grid_spec = pltpu.PrefetchScalarGridSpec(
        num_scalar_prefetch=1,
        grid=(db, nb),
        in_specs=[tok, tok, tok, page_spec(0), page_spec(1), page_spec(0), page_spec(1),
                  pl.BlockSpec((1, HEAD_DIM), lambda b, n, pt_ref: (0, 0))],
        out_specs=tok,
        scratch_shapes=[pltpu.VMEM((nr, width), F32),
                        pltpu.VMEM((nr, HEAD_DIM), F32),
                        pltpu.VMEM((nr, HEAD_DIM), F32),
                        pltpu.VMEM((nr, HEAD_DIM), F32),
                        pltpu.VMEM((nb, nr, width), F32)])
    return pl.pallas_call(
        functools.partial(_moba_sample_kernel, nheads=nheads, nq=nq),
        grid_spec=grid_spec,
        out_shape=jax.ShapeDtypeStruct((db, nq, width), BF16),
        compiler_params=_params(("arbitrary", "arbitrary")),
        name="moba_sample",
    )(pt, q, k_new, v_new, cache_k, cache_k, cache_v, cache_v, gout)


def _outproj_kernel(og_ref, oa_ref, wg_ref, wa_ref, x_ref, gt_ref, y_ref, *, per_row):
    mix = (jnp.dot(og_ref[...], wg_ref[...], preferred_element_type=F32)
           + jnp.dot(oa_ref[...], wa_ref[...], preferred_element_type=F32))
    y_ref[...] = x_ref[...] + _mod_val(gt_ref, per_row) * mix


def _outproj(og, oa, wg, wa, x, gate, *, per_row, rows_per_seq, tm, tn):
    m, d = x.shape
    kw = og.shape[1]
    return pl.pallas_call(
        functools.partial(_outproj_kernel, per_row=per_row),
        grid=(m // tm, d // tn),
        in_specs=[pl.BlockSpec((tm, kw), lambda i, j: (i, 0)),
                  pl.BlockSpec((tm, kw), lambda i, j: (i, 0)),
                  pl.BlockSpec((kw, tn), lambda i, j: (0, j)),
                  pl.BlockSpec((kw, tn), lambda i, j: (0, j)),
                  pl.BlockSpec((tm, tn), lambda i, j: (i, j)),
                  _mod_spec(per_row, tm, tn, rows_per_seq, col_of=lambda j: j)],
        out_specs=pl.BlockSpec((tm, tn), lambda i, j: (i, j)),
        out_shape=jax.ShapeDtypeStruct((m, d), F32),
        compiler_params=_params(("arbitrary", "arbitrary")),
        name="outproj",
    )(og, oa, wg, wa, x, gate)


def _ffn_kernel(x_ref, sc_ref, sh_ref, gt_ref, g_ref, wg_ref, wu_ref, wdw_ref, bdw_ref, wd_ref, st_ref,
                y_ref, cs_ref, h_scr, acc_scr, prev_scr, ext_scr, *, per_row, tiles_per_seq, unit, halo):
    i = pl.program_id(0)
    j = pl.program_id(1)
    tm = x_ref.shape[0]
    keep = (FFN_CONV - 1) * unit

    @pl.when(j == 0)
    def _():
        h = _norm_mod(x_ref[...], g_ref[...], _mod_val(sc_ref, per_row), _mod_val(sh_ref, per_row))
        h_scr[...] = h.astype(BF16)
        acc_scr[...] = jnp.zeros_like(acc_scr)

    hb = h_scr[...]
    gp = jnp.dot(hb, wg_ref[...], preferred_element_type=F32)
    up = jnp.dot(hb, wu_ref[...], preferred_element_type=F32)

    first = (i % tiles_per_seq) == 0

    @pl.when(first)
    def _():
        ext_scr[0:halo] = st_ref[0]

    @pl.when(jnp.logical_not(first))
    def _():
        ext_scr[0:halo] = prev_scr[j]

    ext_scr[halo:halo + tm] = gp
    w = wdw_ref[...]
    gc = w[FFN_CONV - 1:FFN_CONV] * gp + bdw_ref[...]
    for tap in range(FFN_CONV - 1):
        off = halo - (FFN_CONV - 1 - tap) * unit
        gc = gc + w[tap:tap + 1] * ext_scr[off:off + tm]
    prev_scr[j] = ext_scr[tm:tm + halo]
    cs_ref[0] = ext_scr[halo + tm - keep:halo + tm]
    act = _silu(gc) * up
    acc_scr[...] += jnp.dot(act.astype(BF16), wd_ref[...], preferred_element_type=F32)

    @pl.when(j == pl.num_programs(1) - 1)
    def _():
        y_ref[...] = x_ref[...] + _mod_val(gt_ref, per_row) * acc_scr[...]


def _ffn(x, scale, shift, gate, g, wg, wu, wdw, bdw, wd, state, *, per_row, rows_per_seq, unit, tm, tn):
    m, d = x.shape
    dff = wg.shape[1]
    nj = dff // tn
    halo = state.shape[1]
    keep = (FFN_CONV - 1) * unit
    tps = rows_per_seq // tm
    mod = _mod_spec(per_row, tm, d, rows_per_seq)
    return pl.pallas_call(
        functools.partial(_ffn_kernel, per_row=per_row, tiles_per_seq=tps, unit=unit, halo=halo),
        grid=(m // tm, nj),
        in_specs=[pl.BlockSpec((tm, d), lambda i, j: (i, 0)), mod, mod, mod,
                  pl.BlockSpec((1, d), lambda i, j: (0, 0)),
                  pl.BlockSpec((d, tn), lambda i, j: (0, j)),
                  pl.BlockSpec((d, tn), lambda i, j: (0, j)),
                  pl.BlockSpec((FFN_CONV, tn), lambda i, j: (0, j)),
                  pl.BlockSpec((1, tn), lambda i, j: (0, j)),
                  pl.BlockSpec((tn, d), lambda i, j: (j, 0)),
                  pl.BlockSpec((1, halo, tn), lambda i, j: (i // tps, 0, j))],
        out_specs=[pl.BlockSpec((tm, d), lambda i, j: (i, 0)),
                   pl.BlockSpec((1, keep, tn), lambda i, j: (i, 0, j))],
        out_shape=[jax.ShapeDtypeStruct((m, d), F32),
                   jax.ShapeDtypeStruct((m // tm, keep, dff), F32)],
        scratch_shapes=[pltpu.VMEM((tm, d), BF16),
                        pltpu.VMEM((tm, d), F32),
                        pltpu.VMEM((nj, halo, tn), F32),
                        pltpu.VMEM((tm + halo, tn), F32)],
        compiler_params=_params(("arbitrary", "arbitrary")),
        name="ffn",
    )(x, scale, shift, gate, g, wg, wu, wdw, bdw, wd, state)


def _layer_weights(l, w_in, w_gdn_conv, gdn_a_log, gdn_dt_bias, g_gdn_out, g_q, g_k, g_attn_out, w_out,
                   g_norm_mix, g_norm_ffn, w_ffn_gate, w_ffn_up, w_ffn_dw, b_ffn_dw, w_ffn_down):
    nheads = gdn_a_log.shape[1]
    gw = nheads * HEAD_DIM
    o_z, o_a = 3 * gw, 4 * gw
    o_mq = o_a + 2 * nheads
    wi = w_in[l]
    d = wi.shape[0]
    wab = jnp.zeros((d, HEAD_DIM), F32).at[:, :2 * nheads].set(wi[:, o_a:o_mq])
    lane_pad = lambda a: jnp.zeros((1, HEAD_DIM), F32).at[0, :nheads].set(a)
    return dict(
        w_gdn=wi[:, :o_a].astype(BF16),
        w_ab=wab.astype(BF16),
        w_moba=wi[:, o_mq:].astype(BF16),
        qk_gains=jnp.stack([g_q[l], g_k[l]]).reshape(2, 1, HEAD_DIM),
        w_conv=w_gdn_conv[l],
        a_log=lane_pad(gdn_a_log[l]),
        dt_bias=lane_pad(gdn_dt_bias[l]),
        g_gdn_out=g_gdn_out[l].reshape(1, HEAD_DIM),
        g_attn_out=g_attn_out[l].reshape(1, HEAD_DIM),
        w_out_gdn=w_out[l, :gw].astype(BF16),
        w_out_att=w_out[l, gw:].astype(BF16),
        g_mix=g_norm_mix[l].reshape(1, -1),
        g_ffn=g_norm_ffn[l].reshape(1, -1),
        w_gate=w_ffn_gate[l].astype(BF16),
        w_up=w_ffn_up[l].astype(BF16),
        w_dw=w_ffn_dw[l],
        b_dw=b_ffn_dw[l].reshape(1, -1),
        w_down=w_ffn_down[l].astype(BF16),
    )


def _pad_rows_front(a, rows):
    return jnp.pad(a, ((0, 0), (rows - a.shape[1], 0), (0, 0)))


def _prompt_layer(x, mod, lw, *, batch, seq):
    shift1, scale1, gate1, shift2, scale2, gate2 = (mod[:, i][:, None, :] for i in range(6))
    kw = dict(per_row=False, rows_per_seq=seq)
    gw = lw["w_out_gdn"].shape[0]
    dff = lw["w_gate"].shape[1]
    nheads = gw // HEAD_DIM
    qkvz, ab = _inproj_gdn(x, scale1, shift1, lw["g_mix"], lw["w_gdn"], lw["w_ab"], tm=1024, tn=1024, **kw)
    mqkv = _inproj_moba(x, scale1, shift1, lw["g_mix"], lw["w_moba"], lw["qk_gains"], tm=1024, tn=512, **kw)
    o_gdn, s_new, conv_new = _gdn(
        qkvz, ab, jnp.zeros((batch, SUBLANES, 3 * gw), F32), jnp.zeros((batch, nheads, HEAD_DIM, HEAD_DIM), F32),
        lw["w_conv"], lw["a_log"], lw["dt_bias"], lw["g_gdn_out"],
        batch=batch, seq=seq, t_valid=seq, tc=256, chunk=GDN_CHUNK)
    o_att = _moba_prompt(mqkv[0], mqkv[1], mqkv[2], lw["g_attn_out"], batch=batch, seq=seq)
    x = _outproj(o_gdn, o_att, lw["w_out_gdn"], lw["w_out_att"], x, gate1, tm=1024, tn=1024, **kw)
    ffn_tm = 512
    x, tails = _ffn(x, scale2, shift2, gate2, lw["g_ffn"], lw["w_gate"], lw["w_up"], lw["w_dw"], lw["b_dw"],
                    lw["w_down"], jnp.zeros((batch, SUBLANES, dff), F32), unit=1, tm=ffn_tm, tn=512, **kw)
    tps = seq // ffn_tm
    return x, mqkv[1], mqkv[2], s_new, conv_new, tails[tps - 1::tps]


def _sample_layer(x, mod_rows, lw, cache_k, cache_v, page_table, s0, gdn_conv, ffn_conv, *, db, nq):
    shift1, scale1, gate1, shift2, scale2, gate2 = mod_rows
    m = nq * db
    kw = dict(per_row=True, rows_per_seq=m)
    gw = lw["w_out_gdn"].shape[0]
    dff = lw["w_gate"].shape[1]
    to_bm = lambda a: a.reshape(nq, db, -1).transpose(1, 0, 2)
    to_tm = lambda a: a.transpose(1, 0, 2).reshape(m, -1)
    qkvz, ab = _inproj_gdn(x, scale1, shift1, lw["g_mix"], lw["w_gdn"], lw["w_ab"], tm=m, tn=1024, **kw)
    mqkv = _inproj_moba(x, scale1, shift1, lw["g_mix"], lw["w_moba"], lw["qk_gains"], tm=m, tn=512, **kw)
    t_pad = SUBLANES
    pad_t = lambda a: jnp.pad(to_bm(a), ((0, 0), (0, t_pad - nq), (0, 0))).reshape(db * t_pad, -1)
    o_gdn, s_new, conv_new = _gdn(
        pad_t(qkvz), pad_t(ab), _pad_rows_front(gdn_conv, SUBLANES), s0,
        lw["w_conv"], lw["a_log"], lw["dt_bias"], lw["g_gdn_out"],
        batch=db, seq=t_pad, t_valid=nq, tc=t_pad, chunk=t_pad)
    o_gdn = to_tm(o_gdn.reshape(db, t_pad, gw)[:, :nq])
    q_bm, k_bm, v_bm = to_bm(mqkv[0]), to_bm(mqkv[1]), to_bm(mqkv[2])
    o_att = to_tm(_moba_sample(q_bm, k_bm, v_bm, cache_k, cache_v, page_table, lw["g_attn_out"]))
    x = _outproj(o_gdn, o_att, lw["w_out_gdn"], lw["w_out_att"], x, gate1, tm=m, tn=1024, **kw)
    state_tm = ffn_conv.transpose(1, 0, 2).reshape(1, (FFN_CONV - 1) * db, dff)
    x, ffn_new = _ffn(x, scale2, shift2, gate2, lw["g_ffn"], lw["w_gate"], lw["w_up"], lw["w_dw"], lw["b_dw"],
                      lw["w_down"], state_tm, unit=db, tm=m, tn=512, **kw)
    ffn_new = ffn_new.reshape(FFN_CONV - 1, db, dff).transpose(1, 0, 2)
    return x, k_bm, v_bm, s_new, conv_new, ffn_new


def kernel(x_prompt, x_sample, cache_k, cache_v, state_gdn, state_gdn_conv, state_ffn_conv, page_table, c_prompt, c_sample, w_ada, b_ada, g_norm_mix, w_in, w_gdn_conv, gdn_a_log, gdn_dt_bias, g_gdn_out, g_q, g_k, g_attn_out, w_out, g_norm_ffn, w_ffn_gate, w_ffn_up, w_ffn_dw, b_ffn_dw, w_ffn_down):
    depth = w_in.shape[0]
    batch, seq, d = x_prompt.shape
    db, nq, _ = x_sample.shape
    nheads_m = cache_k.shape[3]
    mw = nheads_m * HEAD_DIM

    xp = x_prompt.reshape(batch * seq, d)
    xs = x_sample.transpose(1, 0, 2).reshape(nq * db, d)
    c_all = jnp.concatenate([c_prompt, c_sample], axis=0)

    outs_p = [[] for _ in range(5)]
    outs_s = [[] for _ in range(5)]
    for l in range(depth):
        lw = _layer_weights(l, w_in, w_gdn_conv, gdn_a_log, gdn_dt_bias, g_gdn_out, g_q, g_k, g_attn_out, w_out,
                            g_norm_mix, g_norm_ffn, w_ffn_gate, w_ffn_up, w_ffn_dw, b_ffn_dw, w_ffn_down)
        mod = _ada(c_all, w_ada[l], b_ada[l]).reshape(batch + db, 6, d)
        mod_s = tuple(jnp.tile(mod[batch:, i], (nq, 1)) for i in range(6))

        xp, k_p, v_p, s_p, gc_p, fc_p = _prompt_layer(xp, mod[:batch], lw, batch=batch, seq=seq)
        for acc, val in zip(outs_p, (k_p.reshape(batch, seq, nheads_m, HEAD_DIM),
                                     v_p.reshape(batch, seq, nheads_m, HEAD_DIM), s_p, gc_p, fc_p)):
            acc.append(val)

        ck = cache_k[l].reshape(cache_k.shape[1], cache_k.shape[2], mw)
        cv = cache_v[l].reshape(cache_v.shape[1], cache_v.shape[2], mw)
        xs, k_s, v_s, s_s, gc_s, fc_s = _sample_layer(
            xs, mod_s, lw, ck, cv, page_table, state_gdn[l], state_gdn_conv[l], state_ffn_conv[l], db=db, nq=nq)
        for acc, val in zip(outs_s, (k_s.reshape(db, nq, nheads_m, HEAD_DIM),
                                     v_s.reshape(db, nq, nheads_m, HEAD_DIM), s_s, gc_s, fc_s)):
            acc.append(val)

    y_prompt = xp.reshape(batch, seq, d)
    y_sample = xs.reshape(nq, db, d).transpose(1, 0, 2)
    return (y_prompt, y_sample, *(jnp.stack(a) for a in outs_p), *(jnp.stack(a) for a in outs_s))
```

```python
import functools

import jax
import jax.numpy as jnp
from jax import lax
from jax.experimental import pallas as pl
from jax.experimental.pallas import tpu as pltpu

F32 = jnp.float32
BF16 = jnp.bfloat16

NORM_EPS = 1e-6
HEAD_DIM = 128
GDN_CONV = 4
GDN_CHUNK = 64
MOBA_BLOCK = 256
MOBA_TOPK = 3
FFN_CONV = 3
SUBLANES = 8
VMEM_LIMIT = 56 * 1024 * 1024

_NT = (((1,), (1,)), ((), ()))
_NN = (((1,), (0,)), ((), ()))
_TN = (((0,), (0,)), ((), ()))


def _params(sem):
    return pltpu.CompilerParams(dimension_semantics=sem, vmem_limit_bytes=VMEM_LIMIT)


def _sigmoid(x):
    return 1.0 / (1.0 + jnp.exp(-x))


def _silu(x):
    return x * _sigmoid(x)


def _dot(a, b, dims=_NN):
    return lax.dot_general(a.astype(BF16), b.astype(BF16), dims, preferred_element_type=F32)


def _split(a):
    hi = a.astype(BF16)
    lo = (a - hi.astype(F32)).astype(BF16)
    return hi, lo


def _dot3(a, b, dims=_NN):
    ah, al = _split(a)
    bh, bl = _split(b)
    d = lambda x, y: lax.dot_general(x, y, dims, preferred_element_type=F32)
    return d(ah, bh) + (d(ah, bl) + d(al, bh))


def _norm_mod(x, g, scale, shift):
    ms = jnp.mean(x * x, axis=-1, keepdims=True)
    y = x * lax.rsqrt(ms + NORM_EPS) * g
    return y * (1.0 + scale) + shift


def _head_rmsnorm(x, g):
    ms = jnp.mean(x * x, axis=-1, keepdims=True)
    return x * lax.rsqrt(ms + NORM_EPS) * g


def _ada_kernel(c_ref, w_ref, b_ref, o_ref):
    s = _silu(c_ref[...])
    o_ref[...] = _dot(s, w_ref[...]) + b_ref[...]


def _ada(c, w, b, tn=1024):
    r, d = c.shape
    n = w.shape[1]
    return pl.pallas_call(
        _ada_kernel,
        grid=(n // tn,),
        in_specs=[pl.BlockSpec((r, d), lambda j: (0, 0)),
                  pl.BlockSpec((d, tn), lambda j: (0, j)),
                  pl.BlockSpec((1, tn), lambda j: (0, j))],
        out_specs=pl.BlockSpec((r, tn), lambda j: (0, j)),
        out_shape=jax.ShapeDtypeStruct((r, n), F32),
        compiler_params=_params(("arbitrary",)),
        name="ada",
    )(c, w, b.reshape(1, n))


def _mod_spec(per_row, tm, width, rows_per_seq, col_of=None):
    if col_of is None:
        col_of = lambda j: 0
    if per_row:
        return pl.BlockSpec((tm, width), lambda i, j: (i, col_of(j)))
    tps = rows_per_seq // tm
    return pl.BlockSpec((1, 1, width), lambda i, j: (i // tps, 0, col_of(j)))


def _mod_val(ref, per_row):
    return ref[...] if per_row else ref[0]


def _inproj_gdn_kernel(x_ref, sc_ref, sh_ref, g_ref, w_ref, wab_ref, o_ref, oab_ref, h_scr, *, per_row):
    @pl.when(pl.program_id(1) == 0)
    def _():
        h = _norm_mod(x_ref[...], g_ref[...], _mod_val(sc_ref, per_row), _mod_val(sh_ref, per_row))
        hb = h.astype(BF16)
        h_scr[...] = hb
        oab_ref[...] = jnp.dot(hb, wab_ref[...], preferred_element_type=F32)

    o_ref[...] = jnp.dot(h_scr[...], w_ref[...], preferred_element_type=F32)


def _inproj_gdn(x, scale, shift, g, w, wab, *, per_row, rows_per_seq, tm, tn):
    m, d = x.shape
    n = w.shape[1]
    nab = wab.shape[1]
    mod = _mod_spec(per_row, tm, d, rows_per_seq)
    return pl.pallas_call(
        functools.partial(_inproj_gdn_kernel, per_row=per_row),
        grid=(m // tm, n // tn),
        in_specs=[pl.BlockSpec((tm, d), lambda i, j: (i, 0)), mod, mod,
                  pl.BlockSpec((1, d), lambda i, j: (0, 0)),
                  pl.BlockSpec((d, tn), lambda i, j: (0, j)),
                  pl.BlockSpec((d, nab), lambda i, j: (0, 0))],
        out_specs=[pl.BlockSpec((tm, tn), lambda i, j: (i, j)),
                   pl.BlockSpec((tm, nab), lambda i, j: (i, 0))],
        out_shape=[jax.ShapeDtypeStruct((m, n), F32), jax.ShapeDtypeStruct((m, nab), F32)],
        scratch_shapes=[pltpu.VMEM((tm, d), BF16)],
        compiler_params=_params(("arbitrary", "arbitrary")),
        name="inproj_gdn",
    )(x, scale, shift, g, w, wab)


def _inproj_moba_kernel(x_ref, sc_ref, sh_ref, g_ref, w_ref, gains_ref, o_ref, h_scr, *, per_row, tiles_per_sec):
    j = pl.program_id(1)

    @pl.when(j == 0)
    def _():
        h = _norm_mod(x_ref[...], g_ref[...], _mod_val(sc_ref, per_row), _mod_val(sh_ref, per_row))
        h_scr[...] = h.astype(BF16)

    o_ref[0] = jnp.dot(h_scr[...], w_ref[...], preferred_element_type=F32)
    tm, tn = o_ref.shape[1:]
    row_chunk = min(tm, HEAD_DIM)

    @pl.when(j < 2 * tiles_per_sec)
    def _():
        gain = gains_ref[0]

        def norm_rows(i, carry):
            rows = pl.ds(pl.multiple_of(i * row_chunk, row_chunk), row_chunk)
            for c in range(tn // HEAD_DIM):
                cols = slice(c * HEAD_DIM, (c + 1) * HEAD_DIM)
                o_ref[0, rows, cols] = _head_rmsnorm(o_ref[0, rows, cols], gain)
            return carry

        lax.fori_loop(0, tm // row_chunk, norm_rows, 0)


def _inproj_moba(x, scale, shift, g, w, gains, *, per_row, rows_per_seq, tm, tn):
    m, d = x.shape
    width = w.shape[1] // 3
    tps = width // tn
    mod = _mod_spec(per_row, tm, d, rows_per_seq)
    return pl.pallas_call(
        functools.partial(_inproj_moba_kernel, per_row=per_row, tiles_per_sec=tps),
        grid=(m // tm, 3 * tps),
        in_specs=[pl.BlockSpec((tm, d), lambda i, j: (i, 0)), mod, mod,
                  pl.BlockSpec((1, d), lambda i, j: (0, 0)),
                  pl.BlockSpec((d, tn), lambda i, j: (0, j)),
                  pl.BlockSpec((1, 1, HEAD_DIM), lambda i, j: (jnp.minimum(j // tps, 1), 0, 0))],
        out_specs=pl.BlockSpec((1, tm, tn), lambda i, j: (j // tps, i, j % tps)),
        out_shape=jax.ShapeDtypeStruct((3, m, width), F32),
        scratch_shapes=[pltpu.VMEM((tm, d), BF16)],
        compiler_params=_params(("arbitrary", "arbitrary")),
        name="inproj_moba",
    )(x, scale, shift, g, w, gains)


def _gdn_kernel(qkv_ref, z_ref, ab_ref, cst_ref, s0_ref, wconv_ref, alog_ref, dtb_ref, gout_ref,
                o_ref, sfin_ref, cout_ref,
                s_scr, xe_scr, y_scr, g_scr, beta_scr, *, chunk, t_valid, nheads):
    t = pl.program_id(1)
    nt = pl.num_programs(1)
    tc = qkv_ref.shape[0]
    width = nheads * HEAD_DIM
    halo = SUBLANES

    @pl.when(t == 0)
    def _():
        s_scr[...] = s0_ref[0]
        xe_scr[0:halo] = cst_ref[0]

    x = qkv_ref[...]
    xe_scr[halo:halo + tc] = x
    w = wconv_ref[...]
    y = w[GDN_CONV - 1:GDN_CONV] * x
    for tap in range(GDN_CONV - 1):
        off = halo - (GDN_CONV - 1) + tap
        y = y + w[tap:tap + 1] * xe_scr[off:off + tc]
    y_scr[...] = _silu(y)

    ab = ab_ref[...]
    beta = _sigmoid(ab)
    sp_in = ab + dtb_ref[...]
    softplus = jnp.maximum(sp_in, 0.0) + jnp.log(1.0 + jnp.exp(-jnp.abs(sp_in)))
    g = -jnp.exp(alog_ref[...]) * softplus
    row = lax.broadcasted_iota(jnp.int32, (tc, HEAD_DIM), 0)
    if t_valid % tc != 0:
        valid = (t * tc + row) < t_valid
        beta = jnp.where(valid, beta, 0.0)
        g = jnp.where(valid, g, 0.0)
    in_chunk = row % chunk
    step = 1
    while step < chunk:
        g = g + jnp.where(in_chunk >= step, pltpu.roll(g, step, axis=0), 0.0)
        step *= 2
    g_scr[...] = g
    beta_scr[...] = beta

    ri = lax.broadcasted_iota(jnp.int32, (chunk, chunk), 0)
    ci = lax.broadcasted_iota(jnp.int32, (chunk, chunk), 1)
    causal = ri >= ci
    strict = ri > ci
    eye = jnp.where(ri == ci, 1.0, 0.0).astype(F32)
    gout = gout_ref[...]
    n_square = max(chunk.bit_length() - 2, 0)

    def chunk_body(c, carry):
        rows = pl.ds(pl.multiple_of(c * chunk, chunk), chunk)
        gc = g_scr[rows, :]
        gct = gc.T
        bc = beta_scr[rows, :]
        heads = range(nheads)
        cols = [slice(h * HEAD_DIM, (h + 1) * HEAD_DIM) for h in heads]
        q = [y_scr[rows, cols[h]] for h in heads]
        k = [y_scr[rows, width + h * HEAD_DIM:width + (h + 1) * HEAD_DIM] for h in heads]
        v = [y_scr[rows, 2 * width + h * HEAD_DIM:2 * width + (h + 1) * HEAD_DIM] for h in heads]
        q = [a * lax.rsqrt(jnp.sum(a * a, axis=-1, keepdims=True) + NORM_EPS) * (HEAD_DIM ** -0.5) for a in q]
        k = [a * lax.rsqrt(jnp.sum(a * a, axis=-1, keepdims=True) + NORM_EPS) for a in k]
        g_col = [gc[:, h:h + 1] for h in heads]
        b_col = [bc[:, nheads + h:nheads + h + 1] for h in heads]
        g_end = [gc[chunk - 1:chunk, h:h + 1] for h in heads]
        decay = [jnp.where(causal, jnp.exp(g_col[h] - gct[h:h + 1, :]), 0.0) for h in heads]
        kk = [_dot3(k[h], k[h], _NT) for h in heads]
        qk = [_dot(q[h], k[h], _NT) * decay[h] for h in heads]
        pw = [jnp.where(strict, -(b_col[h] * kk[h] * decay[h]), 0.0) for h in heads]
        inv = [eye + pw[h] for h in heads]
        for _ in range(n_square):
            pw = [_dot3(pw[h], pw[h]) for h in heads]
            inv = [inv[h] + _dot3(inv[h], pw[h]) for h in heads]
        eg = [jnp.exp(g_col[h]) for h in heads]
        sol = [_dot3(inv[h], jnp.concatenate([b_col[h] * v[h], (b_col[h] * eg[h]) * k[h]], axis=-1)) for h in heads]
        s_old = [s_scr[h] for h in heads]
        ws = [_dot(jnp.concatenate([sol[h][:, HEAD_DIM:], q[h] * eg[h]], axis=0), s_old[h]) for h in heads]
        v_new = [sol[h][:, :HEAD_DIM] - ws[h][:chunk] for h in heads]
        o = [ws[h][chunk:] + _dot(qk[h], v_new[h]) for h in heads]
        s_new = [s_old[h] * jnp.exp(g_end[h]) + _dot(k[h] * jnp.exp(g_end[h] - g_col[h]), v_new[h], _TN)
                 for h in heads]
        for h in heads:
            s_scr[h] = s_new[h]
            gated = _head_rmsnorm(o[h], gout) * _silu(z_ref[rows, cols[h]])
            o_ref[rows, cols[h]] = gated.astype(o_ref.dtype)
        return carry

    lax.fori_loop(0, tc // chunk, chunk_body, 0)

    xe_scr[0:halo] = xe_scr[tc:tc + halo]

    @pl.when(t == nt - 1)
    def _():
        sfin_ref[0] = s_scr[...]
        end = halo + t_valid - (t_valid - 1) // tc * tc
        cout_ref[0] = xe_scr[end - (GDN_CONV - 1):end]


def _gdn(qkvz, ab, conv_state8, s0, wconv, alog, dtb, gout, *, batch, seq, t_valid, tc, chunk):
    nheads = s0.shape[1]
    width = nheads * HEAD_DIM
    nt = seq // tc
    assert t_valid >= GDN_CONV - 1 and tc >= SUBLANES and (t_valid - 1) // tc == nt - 1
    return pl.pallas_call(
        functools.partial(_gdn_kernel, chunk=chunk, t_valid=t_valid, nheads=nheads),
        grid=(batch, nt),
        in_specs=[pl.BlockSpec((tc, 3 * width), lambda b, t: (b * nt + t, 0)),
                  pl.BlockSpec((tc, width), lambda b, t: (b * nt + t, 3)),
                  pl.BlockSpec((tc, HEAD_DIM), lambda b, t: (b * nt + t, 0)),
                  pl.BlockSpec((1, SUBLANES, 3 * width), lambda b, t: (b, 0, 0)),
                  pl.BlockSpec((1, nheads, HEAD_DIM, HEAD_DIM), lambda b, t: (b, 0, 0, 0)),
                  pl.BlockSpec((GDN_CONV, 3 * width), lambda b, t: (0, 0)),
                  pl.BlockSpec((1, HEAD_DIM), lambda b, t: (0, 0)),
                  pl.BlockSpec((1, HEAD_DIM), lambda b, t: (0, 0)),
                  pl.BlockSpec((1, HEAD_DIM), lambda b, t: (0, 0))],
        out_specs=[pl.BlockSpec((tc, width), lambda b, t: (b * nt + t, 0)),
                   pl.BlockSpec((1, nheads, HEAD_DIM, HEAD_DIM), lambda b, t: (b, 0, 0, 0)),
                   pl.BlockSpec((1, GDN_CONV - 1, 3 * width), lambda b, t: (b, 0, 0))],
        out_shape=[jax.ShapeDtypeStruct((batch * seq, width), BF16),
                   jax.ShapeDtypeStruct(s0.shape, F32),
                   jax.ShapeDtypeStruct((batch, GDN_CONV - 1, 3 * width), F32)],
        scratch_shapes=[pltpu.VMEM((nheads, HEAD_DIM, HEAD_DIM), F32),
                        pltpu.VMEM((tc + SUBLANES, 3 * width), F32),
                        pltpu.VMEM((tc, 3 * width), F32),
                        pltpu.VMEM((tc, HEAD_DIM), F32),
                        pltpu.VMEM((tc, HEAD_DIM), F32)],
        compiler_params=_params(("arbitrary", "arbitrary")),
        name="gdn",
    )(qkvz, qkvz, ab, conv_state8, s0, wconv, alog, dtb, gout)


def _select_blocks(gate, n_past, axis):
    nb = gate.shape[axis]
    idx = lax.broadcasted_iota(jnp.int32, gate.shape, axis)
    rank = jnp.zeros(gate.shape, F32)
    for m in range(nb):
        gm = gate[m:m + 1, :] if axis == 0 else gate[:, m:m + 1]
        beats = (gm > gate) | ((gm == gate) & (m < idx))
        rank = rank + jnp.where(beats & (m < n_past), 1.0, 0.0)
    return jnp.where((idx < n_past) & (rank < MOBA_TOPK), 1.0, 0.0)


def _moba_prompt_kernel(q_ref, k_ref, v_ref, gout_ref, o_ref, km_scr, vt_scr):
    j = pl.program_id(2)
    blk = MOBA_BLOCK
    nb = k_ref.shape[0] // blk
    scale = HEAD_DIM ** -0.5

    @pl.when(j == 0)
    def _():
        for n in range(nb):
            rows = slice(n * blk, (n + 1) * blk)
            km_scr[n:n + 1, :] = jnp.mean(k_ref[rows, :], axis=0, keepdims=True)
            vt_scr[:, rows] = v_ref[rows, :].T.astype(BF16)

    q = q_ref[...]
    gate_t = _dot3(km_scr[...], q, _NT)
    sel = _select_blocks(gate_t, j, axis=0)
    qb = q.astype(BF16)
    ki = lax.broadcasted_iota(jnp.int32, (blk, blk), 0)
    qi = lax.broadcasted_iota(jnp.int32, (blk, blk), 1)
    tri = jnp.where(ki <= qi, 1.0, 0.0)

    def attend(n_blocks):
        s = []
        for n in range(n_blocks):
            rows = slice(n * blk, (n + 1) * blk)
            sn = lax.dot_general(k_ref[rows, :].astype(BF16), qb, _NT, preferred_element_type=F32) * scale
            keep = jnp.where(j == n, 1.0, 0.0) * tri + sel[n:n + 1, :]
            s.append(jnp.where(keep > 0.0, sn, -jnp.inf))
        m = jnp.max(s[0], axis=0, keepdims=True)
        for sn in s[1:]:
            m = jnp.maximum(m, jnp.max(sn, axis=0, keepdims=True))
        l = jnp.zeros_like(m)
        acc = jnp.zeros((HEAD_DIM, blk), F32)
        for n in range(n_blocks):
            p = jnp.exp(s[n] - m)
            l = l + jnp.sum(p, axis=0, keepdims=True)
            acc = acc + jnp.dot(vt_scr[:, n * blk:(n + 1) * blk], p.astype(BF16), preferred_element_type=F32)
        o = (acc / l).T
        o_ref[...] = _head_rmsnorm(o, gout_ref[...]).astype(o_ref.dtype)

    half = max(nb // 2, 1)
    if half == nb:
        attend(nb)
    else:
        pl.when(j < half)(lambda: attend(half))
        pl.when(j >= half)(lambda: attend(nb))


def _moba_prompt(q, k, v, gout, *, batch, seq):
    width = q.shape[1]
    nheads = width // HEAD_DIM
    nb = seq // MOBA_BLOCK
    assert seq % MOBA_BLOCK == 0
    return pl.pallas_call(
        _moba_prompt_kernel,
        grid=(batch, nheads, nb),
        in_specs=[pl.BlockSpec((MOBA_BLOCK, HEAD_DIM), lambda b, h, j: (b * nb + j, h)),
                  pl.BlockSpec((seq, HEAD_DIM), lambda b, h, j: (b, h)),
                  pl.BlockSpec((seq, HEAD_DIM), lambda b, h, j: (b, h)),
                  pl.BlockSpec((1, HEAD_DIM), lambda b, h, j: (0, 0))],
        out_specs=pl.BlockSpec((MOBA_BLOCK, HEAD_DIM), lambda b, h, j: (b * nb + j, h)),
        out_shape=jax.ShapeDtypeStruct((batch * seq, width), BF16),
        scratch_shapes=[pltpu.VMEM((max(nb, SUBLANES), HEAD_DIM), F32),
                        pltpu.VMEM((HEAD_DIM, seq), BF16)],
        compiler_params=_params(("arbitrary", "arbitrary", "arbitrary")),
        name="moba_prompt",
    )(q, k, v, gout)


def _moba_sample_kernel(pt_ref, q_ref, kn_ref, vn_ref, ka_ref, kb_ref, va_ref, vb_ref, gout_ref, o_ref,
                        bias_scr, m_scr, l_scr, gate_scr, acc_scr, *, nheads, nq):
    del pt_ref
    n = pl.program_id(1)
    nb = pl.num_programs(1)
    nr = nq * nheads
    scale = HEAD_DIM ** -0.5
    page = ka_ref.shape[2]
    nkeys = 2 * page * nheads

    @pl.when(n == 0)
    def _():
        col_head = lax.broadcasted_iota(jnp.int32, (nr, nkeys), 1) % nheads
        row_head = lax.broadcasted_iota(jnp.int32, (nr, nkeys), 0) % nheads
        bias_scr[...] = jnp.where(col_head == row_head, 0.0, -jnp.inf)

    q3 = q_ref[0]
    qb = q3.reshape(nr, HEAD_DIM).astype(BF16)
    k3a, k3b = ka_ref[0, 0], kb_ref[0, 0]
    k_mean = (jnp.sum(k3a, axis=0) + jnp.sum(k3b, axis=0)) / MOBA_BLOCK
    gate = jnp.concatenate([jnp.sum(q3[i] * k_mean, axis=-1, keepdims=True) for i in range(nq)], axis=0)
    flat = lambda a: a.reshape(page * nheads, HEAD_DIM).astype(BF16)
    k2 = jnp.concatenate([flat(k3a), flat(k3b)], axis=0)
    v2 = jnp.concatenate([flat(va_ref[0, 0]), flat(vb_ref[0, 0])], axis=0)
    s = lax.dot_general(qb, k2, _NT, preferred_element_type=F32) * scale + bias_scr[...]
    m = jnp.max(s, axis=-1, keepdims=True)
    p = jnp.exp(s - m)
    l = jnp.sum(p, axis=-1, keepdims=True)
    lane = lax.broadcasted_iota(jnp.int32, (nr, HEAD_DIM), 1)
    here = lane == n
    first = n == 0
    m_scr[...] = jnp.where(here, m, jnp.where(first, 0.0, m_scr[...]))
    l_scr[...] = jnp.where(here, l, jnp.where(first, 0.0, l_scr[...]))
    gate_scr[...] = jnp.where(here, gate, jnp.where(first, 0.0, gate_scr[...]))
    acc_scr[n] = jnp.dot(p.astype(BF16), v2, preferred_element_type=F32)

    @pl.when(n == nb - 1)
    def _():
        n_past = acc_scr.shape[0]
        sel = _select_blocks(gate_scr[:, 0:n_past], n_past, axis=1) > 0.0
        kn2 = kn_ref[0].reshape(nr, HEAD_DIM)
        vn2 = vn_ref[0].reshape(nr, HEAD_DIM)
        s_own = lax.dot_general(qb, kn2.astype(BF16), _NT, preferred_element_type=F32) * scale
        ri = lax.broadcasted_iota(jnp.int32, (nr, nr), 0)
        ci = lax.broadcasted_iota(jnp.int32, (nr, nr), 1)
        own_ok = (ci % nheads == ri % nheads) & (ci // nheads <= ri // nheads)
        s_own = jnp.where(own_ok, s_own, -jnp.inf)
        m_own = jnp.max(s_own, axis=-1, keepdims=True)
        m_all = jnp.where(sel, m_scr[:, 0:n_past], -jnp.inf)
        m_fin = jnp.maximum(m_own, jnp.max(m_all, axis=-1, keepdims=True))
        p_own = jnp.exp(s_own - m_fin)
        wgt = jnp.where(sel, jnp.exp(m_all - m_fin), 0.0)
        l_fin = (jnp.sum(p_own, axis=-1, keepdims=True)
                 + jnp.sum(wgt * l_scr[:, 0:n_past], axis=-1, keepdims=True))
        o = _dot(p_own, vn2)
        for b in range(n_past):
            o = o + wgt[:, b:b + 1] * acc_scr[b]
        o = _head_rmsnorm(o / l_fin, gout_ref[...])
        o_ref[0] = o.reshape(nq, nheads, HEAD_DIM)


def _moba_sample(q, k_new, v_new, cache_k, cache_v, layer, page_table, gout):
    db, nq, nheads, _ = q.shape
    page = cache_k.shape[2]
    n_pages = page_table.shape[1]
    assert MOBA_BLOCK == 2 * page and n_pages % 2 == 0
    nb = n_pages // 2
    nr = nq * nheads
    assert nb <= HEAD_DIM and nheads == SUBLANES
    pt = page_table.reshape(-1)

    def page_spec(half):
        return pl.BlockSpec((1, 1, page, nheads, HEAD_DIM),
                            lambda b, n, pt_ref: (layer, pt_ref[b * n_pages + 2 * n + half], 0, 0, 0))

    tok = pl.BlockSpec((1, nq, nheads, HEAD_DIM), lambda b, n, pt_ref: (b, 0, 0, 0))
    grid_spec = pltpu.PrefetchScalarGridSpec(
        num_scalar_prefetch=1,
        grid=(db, nb),
        in_specs=[tok, tok, tok, page_spec(0), page_spec(1), page_spec(0), page_spec(1),
                  pl.BlockSpec((1, HEAD_DIM), lambda b, n, pt_ref: (0, 0))],
        out_specs=tok,
        scratch_shapes=[pltpu.VMEM((nr, 2 * page * nheads), F32),
                        pltpu.VMEM((nr, HEAD_DIM), F32),
                        pltpu.VMEM((nr, HEAD_DIM), F32),
                        pltpu.VMEM((nr, HEAD_DIM), F32),
                        pltpu.VMEM((nb, nr, HEAD_DIM), F32)])
    return pl.pallas_call(
        functools.partial(_moba_sample_kernel, nheads=nheads, nq=nq),
        grid_spec=grid_spec,
        out_shape=jax.ShapeDtypeStruct((db, nq, nheads, HEAD_DIM), F32),
        compiler_params=_params(("arbitrary", "arbitrary")),
        name="moba_sample",
    )(pt, q, k_new, v_new, cache_k, cache_k, cache_v, cache_v, gout)


def _outproj_kernel(og_ref, oa_ref, wg_ref, wa_ref, x_ref, gt_ref, y_ref, *, per_row):
    mix = (jnp.dot(og_ref[...], wg_ref[...], preferred_element_type=F32)
           + jnp.dot(oa_ref[...], wa_ref[...], preferred_element_type=F32))
    y_ref[...] = x_ref[...] + _mod_val(gt_ref, per_row) * mix


def _outproj(og, oa, wg, wa, x, gate, *, per_row, rows_per_seq, tm, tn):
    m, d = x.shape
    kw = og.shape[1]
    return pl.pallas_call(
        functools.partial(_outproj_kernel, per_row=per_row),
        grid=(m // tm, d // tn),
        in_specs=[pl.BlockSpec((tm, kw), lambda i, j: (i, 0)),
                  pl.BlockSpec((tm, kw), lambda i, j: (i, 0)),
                  pl.BlockSpec((kw, tn), lambda i, j: (0, j)),
                  pl.BlockSpec((kw, tn), lambda i, j: (0, j)),
                  pl.BlockSpec((tm, tn), lambda i, j: (i, j)),
                  _mod_spec(per_row, tm, tn, rows_per_seq, col_of=lambda j: j)],
        out_specs=pl.BlockSpec((tm, tn), lambda i, j: (i, j)),
        out_shape=jax.ShapeDtypeStruct((m, d), F32),
        compiler_params=_params(("arbitrary", "arbitrary")),
        name="outproj",
    )(og, oa, wg, wa, x, gate)


def _ffn_kernel(x_ref, sc_ref, sh_ref, gt_ref, g_ref, wg_ref, wu_ref, wdw_ref, bdw_ref, wd_ref, st_ref,
                y_ref, cs_ref, h_scr, acc_scr, prev_scr, ext_scr, *, per_row, tiles_per_seq, unit, halo):
    i = pl.program_id(0)
    j = pl.program_id(1)
    tm = x_ref.shape[0]
    keep = (FFN_CONV - 1) * unit

    @pl.when(j == 0)
    def _():
        h = _norm_mod(x_ref[...], g_ref[...], _mod_val(sc_ref, per_row), _mod_val(sh_ref, per_row))
        h_scr[...] = h.astype(BF16)
        acc_scr[...] = jnp.zeros_like(acc_scr)

    hb = h_scr[...]
    gp = jnp.dot(hb, wg_ref[...], preferred_element_type=F32)
    up = jnp.dot(hb, wu_ref[...], preferred_element_type=F32)

    first = (i % tiles_per_seq) == 0

    @pl.when(first)
    def _():
        ext_scr[0:halo] = st_ref[0]

    @pl.when(jnp.logical_not(first))
    def _():
        ext_scr[0:halo] = prev_scr[j]

    ext_scr[halo:halo + tm] = gp
    w = wdw_ref[...]
    gc = w[FFN_CONV - 1:FFN_CONV] * gp + bdw_ref[...]
    for tap in range(FFN_CONV - 1):
        off = halo - (FFN_CONV - 1 - tap) * unit
        gc = gc + w[tap:tap + 1] * ext_scr[off:off + tm]
    prev_scr[j] = ext_scr[tm:tm + halo]
    cs_ref[0] = ext_scr[halo + tm - keep:halo + tm]
    act = _silu(gc) * up
    acc_scr[...] += jnp.dot(act.astype(BF16), wd_ref[...], preferred_element_type=F32)

    @pl.when(j == pl.num_programs(1) - 1)
    def _():
        y_ref[...] = x_ref[...] + _mod_val(gt_ref, per_row) * acc_scr[...]


def _ffn(x, scale, shift, gate, g, wg, wu, wdw, bdw, wd, state, *, per_row, rows_per_seq, unit, tm, tn):
    m, d = x.shape
    dff = wg.shape[1]
    nj = dff // tn
    halo = state.shape[1]
    keep = (FFN_CONV - 1) * unit
    tps = rows_per_seq // tm
    mod = _mod_spec(per_row, tm, d, rows_per_seq)
    return pl.pallas_call(
        functools.partial(_ffn_kernel, per_row=per_row, tiles_per_seq=tps, unit=unit, halo=halo),
        grid=(m // tm, nj),
        in_specs=[pl.BlockSpec((tm, d), lambda i, j: (i, 0)), mod, mod, mod,
                  pl.BlockSpec((1, d), lambda i, j: (0, 0)),
                  pl.BlockSpec((d, tn), lambda i, j: (0, j)),
                  pl.BlockSpec((d, tn), lambda i, j: (0, j)),
                  pl.BlockSpec((FFN_CONV, tn), lambda i, j: (0, j)),
                  pl.BlockSpec((1, tn), lambda i, j: (0, j)),
                  pl.BlockSpec((tn, d), lambda i, j: (j, 0)),
                  pl.BlockSpec((1, halo, tn), lambda i, j: (i // tps, 0, j))],
        out_specs=[pl.BlockSpec((tm, d), lambda i, j: (i, 0)),
                   pl.BlockSpec((1, keep, tn), lambda i, j: (i, 0, j))],
        out_shape=[jax.ShapeDtypeStruct((m, d), F32),
                   jax.ShapeDtypeStruct((m // tm, keep, dff), F32)],
        scratch_shapes=[pltpu.VMEM((tm, d), BF16),
                        pltpu.VMEM((tm, d), F32),
                        pltpu.VMEM((nj, halo, tn), F32),
                        pltpu.VMEM((tm + halo, tn), F32)],
        compiler_params=_params(("arbitrary", "arbitrary")),
        name="ffn",
    )(x, scale, shift, gate, g, wg, wu, wdw, bdw, wd, state)


def _layer_weights(l, w_in, w_gdn_conv, gdn_a_log, gdn_dt_bias, g_gdn_out, g_q, g_k, g_attn_out, w_out,
                   g_norm_mix, g_norm_ffn, w_ffn_gate, w_ffn_up, w_ffn_dw, b_ffn_dw, w_ffn_down):
    nheads = gdn_a_log.shape[1]
    gw = nheads * HEAD_DIM
    o_z, o_a = 3 * gw, 4 * gw
    o_mq = o_a + 2 * nheads
    wi = w_in[l]
    d = wi.shape[0]
    wab = jnp.zeros((d, HEAD_DIM), F32).at[:, :2 * nheads].set(wi[:, o_a:o_mq])
    lane_pad = lambda a: jnp.zeros((1, HEAD_DIM), F32).at[0, :nheads].set(a)
    return dict(
        w_gdn=wi[:, :o_a].astype(BF16),
        w_ab=wab.astype(BF16),
        w_moba=wi[:, o_mq:].astype(BF16),
        qk_gains=jnp.stack([g_q[l], g_k[l]]).reshape(2, 1, HEAD_DIM),
        w_conv=w_gdn_conv[l],
        a_log=lane_pad(gdn_a_log[l]),
        dt_bias=lane_pad(gdn_dt_bias[l]),
        g_gdn_out=g_gdn_out[l].reshape(1, HEAD_DIM),
        g_attn_out=g_attn_out[l].reshape(1, HEAD_DIM),
        w_out_gdn=w_out[l, :gw].astype(BF16),
        w_out_att=w_out[l, gw:].astype(BF16),
        g_mix=g_norm_mix[l].reshape(1, -1),
        g_ffn=g_norm_ffn[l].reshape(1, -1),
        w_gate=w_ffn_gate[l].astype(BF16),
        w_up=w_ffn_up[l].astype(BF16),
        w_dw=w_ffn_dw[l],
        b_dw=b_ffn_dw[l].reshape(1, -1),
        w_down=w_ffn_down[l].astype(BF16),
    )


def _pad_rows_front(a, rows):
    return jnp.pad(a, ((0, 0), (rows - a.shape[1], 0), (0, 0)))


def _prompt_layer(x, mod, lw, *, batch, seq):
    shift1, scale1, gate1, shift2, scale2, gate2 = (mod[:, i][:, None, :] for i in range(6))
    kw = dict(per_row=False, rows_per_seq=seq)
    gw = lw["w_out_gdn"].shape[0]
    dff = lw["w_gate"].shape[1]
    nheads = gw // HEAD_DIM
    qkvz, ab = _inproj_gdn(x, scale1, shift1, lw["g_mix"], lw["w_gdn"], lw["w_ab"], tm=1024, tn=1024, **kw)
    mqkv = _inproj_moba(x, scale1, shift1, lw["g_mix"], lw["w_moba"], lw["qk_gains"], tm=1024, tn=512, **kw)
    o_gdn, s_new, conv_new = _gdn(
        qkvz, ab, jnp.zeros((batch, SUBLANES, 3 * gw), F32), jnp.zeros((batch, nheads, HEAD_DIM, HEAD_DIM), F32),
        lw["w_conv"], lw["a_log"], lw["dt_bias"], lw["g_gdn_out"],
        batch=batch, seq=seq, t_valid=seq, tc=256, chunk=GDN_CHUNK)
    o_att = _moba_prompt(mqkv[0], mqkv[1], mqkv[2], lw["g_attn_out"], batch=batch, seq=seq)
    x = _outproj(o_gdn, o_att, lw["w_out_gdn"], lw["w_out_att"], x, gate1, tm=1024, tn=1024, **kw)
    ffn_tm = 512
    x, tails = _ffn(x, scale2, shift2, gate2, lw["g_ffn"], lw["w_gate"], lw["w_up"], lw["w_dw"], lw["b_dw"],
                    lw["w_down"], jnp.zeros((batch, SUBLANES, dff), F32), unit=1, tm=ffn_tm, tn=512, **kw)
    tps = seq // ffn_tm
    return x, mqkv[1], mqkv[2], s_new, conv_new, tails[tps - 1::tps]


def _sample_layer(x, mod_rows, lw, cache_k, cache_v, layer, page_table, s0, gdn_conv, ffn_conv, *, db, nq):
    shift1, scale1, gate1, shift2, scale2, gate2 = mod_rows
    m = nq * db
    kw = dict(per_row=True, rows_per_seq=m)
    gw = lw["w_out_gdn"].shape[0]
    dff = lw["w_gate"].shape[1]
    to_bm = lambda a: a.reshape(nq, db, -1).transpose(1, 0, 2)
    to_tm = lambda a: a.transpose(1, 0, 2).reshape(m, -1)
    qkvz, ab = _inproj_gdn(x, scale1, shift1, lw["g_mix"], lw["w_gdn"], lw["w_ab"], tm=m, tn=1024, **kw)
    mqkv = _inproj_moba(x, scale1, shift1, lw["g_mix"], lw["w_moba"], lw["qk_gains"], tm=m, tn=512, **kw)
    t_pad = SUBLANES
    pad_t = lambda a: jnp.pad(to_bm(a), ((0, 0), (0, t_pad - nq), (0, 0))).reshape(db * t_pad, -1)
    o_gdn, s_new, conv_new = _gdn(
        pad_t(qkvz), pad_t(ab), _pad_rows_front(gdn_conv, SUBLANES), s0,
        lw["w_conv"], lw["a_log"], lw["dt_bias"], lw["g_gdn_out"],
        batch=db, seq=t_pad, t_valid=nq, tc=t_pad, chunk=t_pad)
    o_gdn = to_tm(o_gdn.reshape(db, t_pad, gw)[:, :nq])
    nheads_m = cache_k.shape[3]
    heads = lambda a: to_bm(a).reshape(db, nq, nheads_m, HEAD_DIM)
    q_bm, k_bm, v_bm = heads(mqkv[0]), heads(mqkv[1]), heads(mqkv[2])
    o_att = _moba_sample(q_bm, k_bm, v_bm, cache_k, cache_v, layer, page_table, lw["g_attn_out"])
    o_att = to_tm(o_att.reshape(db, nq, nheads_m * HEAD_DIM)).astype(BF16)
    x = _outproj(o_gdn, o_att, lw["w_out_gdn"], lw["w_out_att"], x, gate1, tm=m, tn=1024, **kw)
    state_tm = ffn_conv.transpose(1, 0, 2).reshape(1, (FFN_CONV - 1) * db, dff)
    x, ffn_new = _ffn(x, scale2, shift2, gate2, lw["g_ffn"], lw["w_gate"], lw["w_up"], lw["w_dw"], lw["b_dw"],
                      lw["w_down"], state_tm, unit=db, tm=m, tn=512, **kw)
    ffn_new = ffn_new.reshape(FFN_CONV - 1, db, dff).transpose(1, 0, 2)
    return x, k_bm, v_bm, s_new, conv_new, ffn_new


def kernel(x_prompt, x_sample, cache_k, cache_v, state_gdn, state_gdn_conv, state_ffn_conv, page_table, c_prompt, c_sample, w_ada, b_ada, g_norm_mix, w_in, w_gdn_conv, gdn_a_log, gdn_dt_bias, g_gdn_out, g_q, g_k, g_attn_out, w_out, g_norm_ffn, w_ffn_gate, w_ffn_up, w_ffn_dw, b_ffn_dw, w_ffn_down):
    depth = w_in.shape[0]
    batch, seq, d = x_prompt.shape
    db, nq, _ = x_sample.shape
    nheads_m = cache_k.shape[3]

    xp = x_prompt.reshape(batch * seq, d)
    xs = x_sample.transpose(1, 0, 2).reshape(nq * db, d)
    c_all = jnp.concatenate([c_prompt, c_sample], axis=0)

    outs_p = [[] for _ in range(5)]
    outs_s = [[] for _ in range(5)]
    for l in range(depth):
        lw = _layer_weights(l, w_in, w_gdn_conv, gdn_a_log, gdn_dt_bias, g_gdn_out, g_q, g_k, g_attn_out, w_out,
                            g_norm_mix, g_norm_ffn, w_ffn_gate, w_ffn_up, w_ffn_dw, b_ffn_dw, w_ffn_down)
        mod = _ada(c_all, w_ada[l], b_ada[l]).reshape(batch + db, 6, d)
        mod_s = tuple(jnp.tile(mod[batch:, i], (nq, 1)) for i in range(6))

        xp, k_p, v_p, s_p, gc_p, fc_p = _prompt_layer(xp, mod[:batch], lw, batch=batch, seq=seq)
        for acc, val in zip(outs_p, (k_p.reshape(batch, seq, nheads_m, HEAD_DIM),
                                     v_p.reshape(batch, seq, nheads_m, HEAD_DIM), s_p, gc_p, fc_p)):
            acc.append(val)

        xs, k_s, v_s, s_s, gc_s, fc_s = _sample_layer(
            xs, mod_s, lw, cache_k, cache_v, l, page_table, state_gdn[l], state_gdn_conv[l], state_ffn_conv[l],
            db=db, nq=nq)
        for acc, val in zip(outs_s, (k_s, v_s, s_s, gc_s, fc_s)):
            acc.append(val)

    y_prompt = xp.reshape(batch, seq, d)
    y_sample = xs.reshape(nq, db, d).transpose(1, 0, 2)
    return (y_prompt, y_sample, *(jnp.stack(a) for a in outs_p), *(jnp.stack(a) for a in outs_s))
```

```python
import functools

import jax
import jax.numpy as jnp
from jax import lax
from jax.experimental import pallas as pl
from jax.experimental.pallas import tpu as pltpu

F32 = jnp.float32
BF16 = jnp.bfloat16

NORM_EPS = 1e-6
HEAD_DIM = 128
GDN_CONV = 4
GDN_CHUNK = 64
MOBA_BLOCK = 256
MOBA_TOPK = 3
FFN_CONV = 3
SUBLANES = 8
MXU_WIDTH = 256
VMEM_LIMIT = 56 * 1024 * 1024

_NT = (((1,), (1,)), ((), ()))
_NN = (((1,), (0,)), ((), ()))
_TN = (((0,), (0,)), ((), ()))


def _params(sem):
    return pltpu.CompilerParams(dimension_semantics=sem, vmem_limit_bytes=VMEM_LIMIT)


def _sigmoid(x):
    return 1.0 / (1.0 + jnp.exp(-x))


def _silu(x):
    return x * _sigmoid(x)


def _dot(a, b, dims=_NN):
    return lax.dot_general(a.astype(BF16), b.astype(BF16), dims, preferred_element_type=F32)


def _split(a):
    hi = a.astype(BF16)
    lo = (a - hi.astype(F32)).astype(BF16)
    return hi, lo


def _dot3(a, b, dims=_NN):
    ah, al = _split(a)
    bh, bl = _split(b)
    d = lambda x, y: lax.dot_general(x, y, dims, preferred_element_type=F32)
    return d(ah, bh) + (d(ah, bl) + d(al, bh))


def _norm_mod(x, g, scale, shift):
    ms = jnp.mean(x * x, axis=-1, keepdims=True)
    y = x * lax.rsqrt(ms + NORM_EPS) * g
    return y * (1.0 + scale) + shift


def _head_rmsnorm(x, g):
    ms = jnp.mean(x * x, axis=-1, keepdims=True)
    return x * lax.rsqrt(ms + NORM_EPS) * g


def _ada_kernel(c_ref, w_ref, b_ref, o_ref):
    s = _silu(c_ref[...])
    o_ref[...] = _dot(s, w_ref[0]) + b_ref[0]


def _ada(c, w, b, layer, tn=1024):
    r, d = c.shape
    n = w.shape[2]
    return pl.pallas_call(
        _ada_kernel,
        grid=(n // tn,),
        in_specs=[pl.BlockSpec((r, d), lambda j: (0, 0)),
                  pl.BlockSpec((1, d, tn), lambda j: (layer, 0, j)),
                  pl.BlockSpec((1, 1, tn), lambda j: (layer, 0, j))],
        out_specs=pl.BlockSpec((r, tn), lambda j: (0, j)),
        out_shape=jax.ShapeDtypeStruct((r, n), F32),
        compiler_params=_params(("arbitrary",)),
        name="ada",
    )(c, w, b.reshape(b.shape[0], 1, n))


def _mod_spec(per_row, tm, width, rows_per_seq, col_of=None):
    if col_of is None:
        col_of = lambda j: 0
    if per_row:
        return pl.BlockSpec((tm, width), lambda i, j: (i, col_of(j)))
    tps = rows_per_seq // tm
    return pl.BlockSpec((1, 1, width), lambda i, j: (i // tps, 0, col_of(j)))


def _mod_val(ref, per_row):
    return ref[...] if per_row else ref[0]


def _inproj_gdn_kernel(x_ref, sc_ref, sh_ref, g_ref, w_ref, wab_ref, o_ref, oab_ref, h_scr, *, per_row):
    @pl.when(pl.program_id(1) == 0)
    def _():
        h = _norm_mod(x_ref[...], g_ref[...], _mod_val(sc_ref, per_row), _mod_val(sh_ref, per_row))
        hb = h.astype(BF16)
        h_scr[...] = hb
        oab_ref[...] = jnp.dot(hb, wab_ref[...], preferred_element_type=F32)

    o_ref[...] = jnp.dot(h_scr[...], w_ref[...], preferred_element_type=F32)


def _inproj_gdn(x, scale, shift, g, w, wab, *, per_row, rows_per_seq, tm, tn):
    m, d = x.shape
    n = w.shape[1]
    nab = wab.shape[1]
    mod = _mod_spec(per_row, tm, d, rows_per_seq)
    return pl.pallas_call(
        functools.partial(_inproj_gdn_kernel, per_row=per_row),
        grid=(m // tm, n // tn),
        in_specs=[pl.BlockSpec((tm, d), lambda i, j: (i, 0)), mod, mod,
                  pl.BlockSpec((1, d), lambda i, j: (0, 0)),
                  pl.BlockSpec((d, tn), lambda i, j: (0, j)),
                  pl.BlockSpec((d, nab), lambda i, j: (0, 0))],
        out_specs=[pl.BlockSpec((tm, tn), lambda i, j: (i, j)),
                   pl.BlockSpec((tm, nab), lambda i, j: (i, 0))],
        out_shape=[jax.ShapeDtypeStruct((m, n), F32), jax.ShapeDtypeStruct((m, nab), F32)],
        scratch_shapes=[pltpu.VMEM((tm, d), BF16)],
        compiler_params=_params(("arbitrary", "arbitrary")),
        name="inproj_gdn",
    )(x, scale, shift, g, w, wab)


def _inproj_moba_kernel(x_ref, sc_ref, sh_ref, g_ref, w_ref, gains_ref, o_ref, h_scr, *, per_row, tiles_per_sec):
    j = pl.program_id(1)

    @pl.when(j == 0)
    def _():
        h = _norm_mod(x_ref[...], g_ref[...], _mod_val(sc_ref, per_row), _mod_val(sh_ref, per_row))
        h_scr[...] = h.astype(BF16)

    r = jnp.dot(h_scr[...], w_ref[...], preferred_element_type=F32)
    tn = r.shape[1]

    @pl.when(j < 2 * tiles_per_sec)
    def _():
        gain = gains_ref[0]
        for c in range(tn // HEAD_DIM):
            cols = slice(c * HEAD_DIM, (c + 1) * HEAD_DIM)
            o_ref[0, :, cols] = _head_rmsnorm(r[:, cols], gain)

    @pl.when(j >= 2 * tiles_per_sec)
    def _():
        o_ref[0] = r


def _inproj_moba(x, scale, shift, g, w, gains, *, per_row, rows_per_seq, tm, tn):
    m, d = x.shape
    width = w.shape[1] // 3
    tps = width // tn
    mod = _mod_spec(per_row, tm, d, rows_per_seq)
    return pl.pallas_call(
        functools.partial(_inproj_moba_kernel, per_row=per_row, tiles_per_sec=tps),
        grid=(m // tm, 3 * tps),
        in_specs=[pl.BlockSpec((tm, d), lambda i, j: (i, 0)), mod, mod,
                  pl.BlockSpec((1, d), lambda i, j: (0, 0)),
                  pl.BlockSpec((d, tn), lambda i, j: (0, j)),
                  pl.BlockSpec((1, 1, HEAD_DIM), lambda i, j: (jnp.minimum(j // tps, 1), 0, 0))],
        out_specs=pl.BlockSpec((1, tm, tn), lambda i, j: (j // tps, i, j % tps)),
        out_shape=jax.ShapeDtypeStruct((3, m, width), F32),
        scratch_shapes=[pltpu.VMEM((tm, d), BF16)],
        compiler_params=_params(("arbitrary", "arbitrary")),
        name="inproj_moba",
    )(x, scale, shift, g, w, gains)


def _gdn_kernel(qkv_ref, z_ref, ab_ref, cst_ref, s0_ref, wconv_ref, alog_ref, dtb_ref, gout_ref,
                o_ref, sfin_ref, cout_ref,
                s_scr, xe_scr, y_scr, g_scr, beta_scr, *, chunk, t_valid, nheads):
    t = pl.program_id(1)
    nt = pl.num_programs(1)
    tc = qkv_ref.shape[0]
    width = nheads * HEAD_DIM
    halo = SUBLANES

    @pl.when(t == 0)
    def _():
        s_scr[...] = s0_ref[0, 0]
        xe_scr[0:halo] = cst_ref[0]

    x = qkv_ref[...]
    xe_scr[halo:halo + tc] = x
    w = wconv_ref[...]
    y = w[GDN_CONV - 1:GDN_CONV] * x
    for tap in range(GDN_CONV - 1):
        off = halo - (GDN_CONV - 1) + tap
        y = y + w[tap:tap + 1] * xe_scr[off:off + tc]
    y_scr[...] = _silu(y)

    ab = ab_ref[...]
    beta = _sigmoid(ab)
    sp_in = ab + dtb_ref[...]
    softplus = jnp.maximum(sp_in, 0.0) + jnp.log(1.0 + jnp.exp(-jnp.abs(sp_in)))
    g = -jnp.exp(alog_ref[...]) * softplus
    row = lax.broadcasted_iota(jnp.int32, (tc, HEAD_DIM), 0)
    if t_valid % tc != 0:
        valid = (t * tc + row) < t_valid
        beta = jnp.where(valid, beta, 0.0)
        g = jnp.where(valid, g, 0.0)
    in_chunk = row % chunk
    step = 1
    while step < chunk:
        g = g + jnp.where(in_chunk >= step, pltpu.roll(g, step, axis=0), 0.0)
        step *= 2
    g_scr[...] = g
    beta_scr[...] = beta

    ri = lax.broadcasted_iota(jnp.int32, (chunk, chunk), 0)
    ci = lax.broadcasted_iota(jnp.int32, (chunk, chunk), 1)
    causal = ri >= ci
    strict = ri > ci
    eye = jnp.where(ri == ci, 1.0, 0.0).astype(F32)
    gout = gout_ref[...]
    n_square = max(chunk.bit_length() - 2, 0)

    def chunk_body(c, carry):
        rows = pl.ds(pl.multiple_of(c * chunk, chunk), chunk)
        gc = g_scr[rows, :]
        gct = gc.T
        bc = beta_scr[rows, :]
        heads = range(nheads)
        cols = [slice(h * HEAD_DIM, (h + 1) * HEAD_DIM) for h in heads]
        q = [y_scr[rows, cols[h]] for h in heads]
        k = [y_scr[rows, width + h * HEAD_DIM:width + (h + 1) * HEAD_DIM] for h in heads]
        v = [y_scr[rows, 2 * width + h * HEAD_DIM:2 * width + (h + 1) * HEAD_DIM] for h in heads]
        q = [a * lax.rsqrt(jnp.sum(a * a, axis=-1, keepdims=True) + NORM_EPS) * (HEAD_DIM ** -0.5) for a in q]
        k = [a * lax.rsqrt(jnp.sum(a * a, axis=-1, keepdims=True) + NORM_EPS) for a in k]
        g_col = [gc[:, h:h + 1] for h in heads]
        b_col = [bc[:, nheads + h:nheads + h + 1] for h in heads]
        g_end = [gc[chunk - 1:chunk, h:h + 1] for h in heads]
        decay = [jnp.where(causal, jnp.exp(g_col[h] - gct[h:h + 1, :]), 0.0) for h in heads]
        kk = [_dot3(k[h], k[h], _NT) for h in heads]
        qk = [_dot(q[h], k[h], _NT) * decay[h] for h in heads]
        pw = [jnp.where(strict, -(b_col[h] * kk[h] * decay[h]), 0.0) for h in heads]
        inv = [eye + pw[h] for h in heads]
        for _ in range(n_square):
            pw = [_dot3(pw[h], pw[h]) for h in heads]
            inv = [inv[h] + _dot3(inv[h], pw[h]) for h in heads]
        eg = [jnp.exp(g_col[h]) for h in heads]
        sol = [_dot3(inv[h], jnp.concatenate([b_col[h] * v[h], (b_col[h] * eg[h]) * k[h]], axis=-1)) for h in heads]
        s_old = [s_scr[h] for h in heads]
        ws = [_dot(jnp.concatenate([sol[h][:, HEAD_DIM:], q[h] * eg[h]], axis=0), s_old[h]) for h in heads]
        v_new = [sol[h][:, :HEAD_DIM] - ws[h][:chunk] for h in heads]
        o = [ws[h][chunk:] + _dot(qk[h], v_new[h]) for h in heads]
        s_new = [s_old[h] * jnp.exp(g_end[h]) + _dot(k[h] * jnp.exp(g_end[h] - g_col[h]), v_new[h], _TN)
                 for h in heads]
        for h in heads:
            s_scr[h] = s_new[h]
            gated = _head_rmsnorm(o[h], gout) * _silu(z_ref[rows, cols[h]])
            o_ref[rows, cols[h]] = gated.astype(o_ref.dtype)
        return carry

    lax.fori_loop(0, tc // chunk, chunk_body, 0)

    xe_scr[0:halo] = xe_scr[tc:tc + halo]

    @pl.when(t == nt - 1)
    def _():
        sfin_ref[0] = s_scr[...]
        end = halo + t_valid - (t_valid - 1) // tc * tc
        cout_ref[0] = xe_scr[end - (GDN_CONV - 1):end]


def _gdn(qkvz, ab, conv_state8, s0, layer, wconv, alog, dtb, gout, *, batch, seq, t_valid, tc, chunk):
    nheads = s0.shape[2]
    width = nheads * HEAD_DIM
    nt = seq // tc
    assert t_valid >= GDN_CONV - 1 and tc >= SUBLANES and (t_valid - 1) // tc == nt - 1
    return pl.pallas_call(
        functools.partial(_gdn_kernel, chunk=chunk, t_valid=t_valid, nheads=nheads),
        grid=(batch, nt),
        in_specs=[pl.BlockSpec((tc, 3 * width), lambda b, t: (b * nt + t, 0)),
                  pl.BlockSpec((tc, width), lambda b, t: (b * nt + t, 3)),
                  pl.BlockSpec((tc, HEAD_DIM), lambda b, t: (b * nt + t, 0)),
                  pl.BlockSpec((1, SUBLANES, 3 * width), lambda b, t: (b, 0, 0)),
                  pl.BlockSpec((1, 1, nheads, HEAD_DIM, HEAD_DIM), lambda b, t: (layer, b, 0, 0, 0)),
                  pl.BlockSpec((GDN_CONV, 3 * width), lambda b, t: (0, 0)),
                  pl.BlockSpec((1, HEAD_DIM), lambda b, t: (0, 0)),
                  pl.BlockSpec((1, HEAD_DIM), lambda b, t: (0, 0)),
                  pl.BlockSpec((1, HEAD_DIM), lambda b, t: (0, 0))],
        out_specs=[pl.BlockSpec((tc, width), lambda b, t: (b * nt + t, 0)),
                   pl.BlockSpec((1, nheads, HEAD_DIM, HEAD_DIM), lambda b, t: (b, 0, 0, 0)),
                   pl.BlockSpec((1, GDN_CONV - 1, 3 * width), lambda b, t: (b, 0, 0))],
        out_shape=[jax.ShapeDtypeStruct((batch * seq, width), BF16),
                   jax.ShapeDtypeStruct(s0.shape[1:], F32),
                   jax.ShapeDtypeStruct((batch, GDN_CONV - 1, 3 * width), F32)],
        scratch_shapes=[pltpu.VMEM((nheads, HEAD_DIM, HEAD_DIM), F32),
                        pltpu.VMEM((tc + SUBLANES, 3 * width), F32),
                        pltpu.VMEM((tc, 3 * width), F32),
                        pltpu.VMEM((tc, HEAD_DIM), F32),
                        pltpu.VMEM((tc, HEAD_DIM), F32)],
        compiler_params=_params(("arbitrary", "arbitrary")),
        name="gdn",
    )(qkvz, qkvz, ab, conv_state8, s0, wconv, alog, dtb, gout)


def _select_blocks(gate, n_past, axis):
    nb = gate.shape[axis]
    idx = lax.broadcasted_iota(jnp.int32, gate.shape, axis)
    rank = jnp.zeros(gate.shape, F32)
    for m in range(nb):
        gm = gate[m:m + 1, :] if axis == 0 else gate[:, m:m + 1]
        beats = (gm > gate) | ((gm == gate) & (m < idx))
        rank = rank + jnp.where(beats & (m < n_past), 1.0, 0.0)
    return jnp.where((idx < n_past) & (rank < MOBA_TOPK), 1.0, 0.0)


def _moba_prompt_kernel(q_ref, k_ref, v_ref, gout_ref, o_ref, km_scr, vt_scr):
    j = pl.program_id(2)
    blk = MOBA_BLOCK
    nb = k_ref.shape[1] // blk
    scale = HEAD_DIM ** -0.5

    @pl.when(j == 0)
    def _():
        for n in range(nb):
            rows = slice(n * blk, (n + 1) * blk)
            km_scr[n:n + 1, :] = jnp.mean(k_ref[0, rows, :], axis=0, keepdims=True)
            vt_scr[:, rows] = v_ref[0, rows, :].T.astype(BF16)

    q = q_ref[0]
    gate_t = _dot3(km_scr[...], q, _NT)
    sel_bias = jnp.where(_select_blocks(gate_t, j, axis=0) > 0.0, 0.0, -jnp.inf)
    qb = q.astype(BF16)
    ki = lax.broadcasted_iota(jnp.int32, (blk, blk), 0)
    qi = lax.broadcasted_iota(jnp.int32, (blk, blk), 1)
    tri_bias = jnp.where(ki <= qi, 0.0, -jnp.inf)

    def attend(own):
        s = []
        for n in range(own + 1):
            rows = slice(n * blk, (n + 1) * blk)
            sn = lax.dot_general(k_ref[0, rows, :].astype(BF16), qb, _NT, preferred_element_type=F32) * scale
            s.append(sn + (tri_bias if n == own else sel_bias[n:n + 1, :]))
        m = jnp.max(s[own], axis=0, keepdims=True)
        for sn in s[:own]:
            m = jnp.maximum(m, jnp.max(sn, axis=0, keepdims=True))
        l = jnp.zeros_like(m)
        acc = jnp.zeros((HEAD_DIM, blk), F32)
        for n in range(own + 1):
            p = jnp.exp(s[n] - m)
            l = l + jnp.sum(p, axis=0, keepdims=True)
            acc = acc + jnp.dot(vt_scr[:, n * blk:(n + 1) * blk], p.astype(BF16), preferred_element_type=F32)
        o = (acc / l).T
        o_ref[...] = _head_rmsnorm(o, gout_ref[...]).astype(o_ref.dtype)

    for own in range(nb):
        pl.when(j == own)(functools.partial(attend, own))


def _moba_prompt(qkv, gout, *, batch, seq):
    width = qkv.shape[2]
    nheads = width // HEAD_DIM
    nb = seq // MOBA_BLOCK
    assert seq % MOBA_BLOCK == 0
    return pl.pallas_call(
        _moba_prompt_kernel,
        grid=(batch, nheads, nb),
        in_specs=[pl.BlockSpec((1, MOBA_BLOCK, HEAD_DIM), lambda b, h, j: (0, b * nb + j, h)),
                  pl.BlockSpec((1, seq, HEAD_DIM), lambda b, h, j: (1, b, h)),
                  pl.BlockSpec((1, seq, HEAD_DIM), lambda b, h, j: (2, b, h)),
                  pl.BlockSpec((1, HEAD_DIM), lambda b, h, j: (0, 0))],
        out_specs=pl.BlockSpec((MOBA_BLOCK, HEAD_DIM), lambda b, h, j: (b * nb + j, h)),
        out_shape=jax.ShapeDtypeStruct((batch * seq, width), BF16),
        scratch_shapes=[pltpu.VMEM((max(nb, SUBLANES), HEAD_DIM), F32),
                        pltpu.VMEM((HEAD_DIM, seq), BF16)],
        compiler_params=_params(("arbitrary", "arbitrary", "arbitrary")),
        name="moba_prompt",
    )(qkv, qkv, qkv, gout)


def _moba_sample_kernel(pt_ref, q_ref, kn_ref, vn_ref, ka_ref, kb_ref, va_ref, vb_ref, gout_ref, o_ref,
                        bias_scr, m_scr, l_scr, gate_scr, acc_scr, *, nheads, nq):
    del pt_ref
    n = pl.program_id(1)
    nb = pl.num_programs(1)
    nr = nq * nheads
    scale = HEAD_DIM ** -0.5
    page = ka_ref.shape[2]
    nkeys = 2 * page * nheads

    @pl.when(n == 0)
    def _():
        col_head = lax.broadcasted_iota(jnp.int32, (nr, nkeys), 1) % nheads
        row_head = lax.broadcasted_iota(jnp.int32, (nr, nkeys), 0) % nheads
        bias_scr[...] = jnp.where(col_head == row_head, 0.0, -jnp.inf)

    q3 = q_ref[0]
    qb = q3.reshape(nr, HEAD_DIM).astype(BF16)
    k3a, k3b = ka_ref[0, 0], kb_ref[0, 0]
    k_mean = (jnp.sum(k3a, axis=0) + jnp.sum(k3b, axis=0)) / MOBA_BLOCK
    gate = jnp.concatenate([jnp.sum(q3[i] * k_mean, axis=-1, keepdims=True) for i in range(nq)], axis=0)
    flat = lambda a: a.reshape(page * nheads, HEAD_DIM).astype(BF16)
    k2 = jnp.concatenate([flat(k3a), flat(k3b)], axis=0)
    v2 = jnp.concatenate([flat(va_ref[0, 0]), flat(vb_ref[0, 0])], axis=0)
    s = lax.dot_general(qb, k2, _NT, preferred_element_type=F32) * scale + bias_scr[...]
    m = jnp.max(s, axis=-1, keepdims=True)
    p = jnp.exp(s - m)
    l = jnp.sum(p, axis=-1, keepdims=True)
    lane = lax.broadcasted_iota(jnp.int32, (nr, HEAD_DIM), 1)
    here = lane == n
    first = n == 0
    m_scr[...] = jnp.where(here, m, jnp.where(first, 0.0, m_scr[...]))
    l_scr[...] = jnp.where(here, l, jnp.where(first, 0.0, l_scr[...]))
    gate_scr[...] = jnp.where(here, gate, jnp.where(first, 0.0, gate_scr[...]))
    acc_scr[n] = jnp.dot(p.astype(BF16), v2, preferred_element_type=F32)

    @pl.when(n == nb - 1)
    def _():
        n_past = acc_scr.shape[0]
        sel = _select_blocks(gate_scr[:, 0:n_past], n_past, axis=1) > 0.0
        kn2 = kn_ref[0].reshape(nr, HEAD_DIM)
        vn2 = vn_ref[0].reshape(nr, HEAD_DIM)
        s_own = lax.dot_general(qb, kn2.astype(BF16), _NT, preferred_element_type=F32) * scale
        ri = lax.broadcasted_iota(jnp.int32, (nr, nr), 0)
        ci = lax.broadcasted_iota(jnp.int32, (nr, nr), 1)
        own_ok = (ci % nheads == ri % nheads) & (ci // nheads <= ri // nheads)
        s_own = jnp.where(own_ok, s_own, -jnp.inf)
        m_own = jnp.max(s_own, axis=-1, keepdims=True)
        m_all = jnp.where(sel, m_scr[:, 0:n_past], -jnp.inf)
        m_fin = jnp.maximum(m_own, jnp.max(m_all, axis=-1, keepdims=True))
        p_own = jnp.exp(s_own - m_fin)
        wgt = jnp.where(sel, jnp.exp(m_all - m_fin), 0.0)
        l_fin = (jnp.sum(p_own, axis=-1, keepdims=True)
                 + jnp.sum(wgt * l_scr[:, 0:n_past], axis=-1, keepdims=True))
        o = _dot(p_own, vn2)
        for b in range(n_past):
            o = o + wgt[:, b:b + 1] * acc_scr[b]
        o = _head_rmsnorm(o / l_fin, gout_ref[...])
        o_ref[0] = o.reshape(nq, nheads, HEAD_DIM)


def _moba_sample(q, k_new, v_new, cache_k, cache_v, layer, page_table, gout):
    db, nq, nheads, _ = q.shape
    page = cache_k.shape[2]
    n_pages = page_table.shape[1]
    assert MOBA_BLOCK == 2 * page and n_pages % 2 == 0
    nb = n_pages // 2
    nr = nq * nheads
    assert nb <= HEAD_DIM and nheads == SUBLANES
    pt = page_table.reshape(-1)

    def page_spec(half):
        return pl.BlockSpec((1, 1, page, nheads, HEAD_DIM),
                            lambda b, n, pt_ref: (layer, pt_ref[b * n_pages + 2 * n + half], 0, 0, 0))

    tok = pl.BlockSpec((1, nq, nheads, HEAD_DIM), lambda b, n, pt_ref: (b, 0, 0, 0))
    grid_spec = pltpu.PrefetchScalarGridSpec(
        num_scalar_prefetch=1,
        grid=(db, nb),
        in_specs=[tok, tok, tok, page_spec(0), page_spec(1), page_spec(0), page_spec(1),
                  pl.BlockSpec((1, HEAD_DIM), lambda b, n, pt_ref: (0, 0))],
        out_specs=tok,
        scratch_shapes=[pltpu.VMEM((nr, 2 * page * nheads), F32),
                        pltpu.VMEM((nr, HEAD_DIM), F32),
                        pltpu.VMEM((nr, HEAD_DIM), F32),
                        pltpu.VMEM((nr, HEAD_DIM), F32),
                        pltpu.VMEM((nb, nr, HEAD_DIM), F32)])
    return pl.pallas_call(
        functools.partial(_moba_sample_kernel, nheads=nheads, nq=nq),
        grid_spec=grid_spec,
        out_shape=jax.ShapeDtypeStruct((db, nq, nheads, HEAD_DIM), F32),
        compiler_params=_params(("arbitrary", "arbitrary")),
        name="moba_sample",
    )(pt, q, k_new, v_new, cache_k, cache_k, cache_v, cache_v, gout)


def _outproj_kernel(og_ref, oa_ref, wg_ref, wa_ref, x_ref, gt_ref, y_ref, *, per_row):
    mix = (jnp.dot(og_ref[...], wg_ref[...], preferred_element_type=F32)
           + jnp.dot(oa_ref[...], wa_ref[...], preferred_element_type=F32))
    y_ref[...] = x_ref[...] + _mod_val(gt_ref, per_row) * mix


def _outproj(og, oa, wg, wa, x, gate, *, per_row, rows_per_seq, tm, tn):
    m, d = x.shape
    kw = og.shape[1]
    return pl.pallas_call(
        functools.partial(_outproj_kernel, per_row=per_row),
        grid=(m // tm, d // tn),
        in_specs=[pl.BlockSpec((tm, kw), lambda i, j: (i, 0)),
                  pl.BlockSpec((tm, kw), lambda i, j: (i, 0)),
                  pl.BlockSpec((kw, tn), lambda i, j: (0, j)),
                  pl.BlockSpec((kw, tn), lambda i, j: (0, j)),
                  pl.BlockSpec((tm, tn), lambda i, j: (i, j)),
                  _mod_spec(per_row, tm, tn, rows_per_seq, col_of=lambda j: j)],
        out_specs=pl.BlockSpec((tm, tn), lambda i, j: (i, j)),
        out_shape=jax.ShapeDtypeStruct((m, d), F32),
        compiler_params=_params(("arbitrary", "arbitrary")),
        name="outproj",
    )(og, oa, wg, wa, x, gate)


def _ffn_kernel(x_ref, sc_ref, sh_ref, gt_ref, g_ref, wg_ref, wu_ref, wdw_ref, bdw_ref, wd_ref, st_ref,
                y_ref, cs_ref, h_scr, acc_scr, prev_scr, ext_scr, *, per_row, tiles_per_seq, unit, halo):
    i = pl.program_id(0)
    j = pl.program_id(1)
    tm = x_ref.shape[0]
    keep = (FFN_CONV - 1) * unit

    @pl.when(j == 0)
    def _():
        h = _norm_mod(x_ref[...], g_ref[...], _mod_val(sc_ref, per_row), _mod_val(sh_ref, per_row))
        h_scr[...] = h.astype(BF16)
        acc_scr[...] = jnp.zeros_like(acc_scr)

    hb = h_scr[...]
    first = (i % tiles_per_seq) == 0

    @pl.when(first)
    def _():
        ext_scr[0:halo] = st_ref[0]

    @pl.when(jnp.logical_not(first))
    def _():
        ext_scr[0:halo] = prev_scr[j]

    tn = wg_ref.shape[1]
    n_split = 2 if tn % (2 * MXU_WIDTH) == 0 else 1
    cw = tn // n_split
    parts = [slice(s * cw, (s + 1) * cw) for s in range(n_split)]
    w = wdw_ref[...]
    bias = bdw_ref[...]
    gps = [jnp.dot(hb, wg_ref[:, c], preferred_element_type=F32) for c in parts]
    ups = [jnp.dot(hb, wu_ref[:, c], preferred_element_type=F32) for c in parts]
    acts = []
    for c, gp, up in zip(parts, gps, ups):
        ext_scr[halo:halo + tm, c] = gp
        gc = w[FFN_CONV - 1:FFN_CONV, c] * gp + bias[:, c]
        for tap in range(FFN_CONV - 1):
            off = halo - (FFN_CONV - 1 - tap) * unit
            gc = gc + w[tap:tap + 1, c] * ext_scr[off:off + tm, c]
        acts.append((_silu(gc) * up).astype(BF16))
    prev_scr[j] = ext_scr[tm:tm + halo]
    cs_ref[0] = ext_scr[halo + tm - keep:halo + tm]
    down = jnp.dot(acts[0], wd_ref[parts[0], :], preferred_element_type=F32)
    for a, c in zip(acts[1:], parts[1:]):
        down = down + jnp.dot(a, wd_ref[c, :], preferred_element_type=F32)
    acc_scr[...] += down

    @pl.when(j == pl.num_programs(1) - 1)
    def _():
        y_ref[...] = x_ref[...] + _mod_val(gt_ref, per_row) * acc_scr[...]


def _ffn(x, scale, shift, gate, g, wg, wu, wdw, bdw, wd, state, *, per_row, rows_per_seq, unit, tm, tn):
    m, d = x.shape
    dff = wg.shape[1]
    nj = dff // tn
    halo = state.shape[1]
    keep = (FFN_CONV - 1) * unit
    tps = rows_per_seq // tm
    mod = _mod_spec(per_row, tm, d, rows_per_seq)
    return pl.pallas_call(
        functools.partial(_ffn_kernel, per_row=per_row, tiles_per_seq=tps, unit=unit, halo=halo),
        grid=(m // tm, nj),
        in_specs=[pl.BlockSpec((tm, d), lambda i, j: (i, 0)), mod, mod, mod,
                  pl.BlockSpec((1, d), lambda i, j: (0, 0)),
                  pl.BlockSpec((d, tn), lambda i, j: (0, j)),
                  pl.BlockSpec((d, tn), lambda i, j: (0, j)),
                  pl.BlockSpec((FFN_CONV, tn), lambda i, j: (0, j)),
                  pl.BlockSpec((1, tn), lambda i, j: (0, j)),
                  pl.BlockSpec((tn, d), lambda i, j: (j, 0)),
                  pl.BlockSpec((1, halo, tn), lambda i, j: (i // tps, 0, j))],
        out_specs=[pl.BlockSpec((tm, d), lambda i, j: (i, 0)),
                   pl.BlockSpec((1, keep, tn), lambda i, j: (i, 0, j))],
        out_shape=[jax.ShapeDtypeStruct((m, d), F32),
                   jax.ShapeDtypeStruct((m // tm, keep, dff), F32)],
        scratch_shapes=[pltpu.VMEM((tm, d), BF16),
                        pltpu.VMEM((tm, d), F32),
                        pltpu.VMEM((nj, halo, tn), F32),
                        pltpu.VMEM((tm + halo, tn), F32)],
        compiler_params=_params(("arbitrary", "arbitrary")),
        name="ffn",
    )(x, scale, shift, gate, g, wg, wu, wdw, bdw, wd, state)


def _layer_weights(l, w_in, w_gdn_conv, gdn_a_log, gdn_dt_bias, g_gdn_out, g_q, g_k, g_attn_out, w_out,
                   g_norm_mix, g_norm_ffn, w_ffn_gate, w_ffn_up, w_ffn_dw, b_ffn_dw, w_ffn_down):
    nheads = gdn_a_log.shape[1]
    gw = nheads * HEAD_DIM
    o_z, o_a = 3 * gw, 4 * gw
    o_mq = o_a + 2 * nheads
    wi = w_in[l]
    d = wi.shape[0]
    wab = jnp.zeros((d, HEAD_DIM), F32).at[:, :2 * nheads].set(wi[:, o_a:o_mq])
    lane_pad = lambda a: jnp.zeros((1, HEAD_DIM), F32).at[0, :nheads].set(a)
    return dict(
        w_gdn=wi[:, :o_a].astype(BF16),
        w_ab=wab.astype(BF16),
        w_moba=wi[:, o_mq:].astype(BF16),
        qk_gains=jnp.stack([g_q[l], g_k[l]]).reshape(2, 1, HEAD_DIM),
        w_conv=w_gdn_conv[l],
        a_log=lane_pad(gdn_a_log[l]),
        dt_bias=lane_pad(gdn_dt_bias[l]),
        g_gdn_out=g_gdn_out[l].reshape(1, HEAD_DIM),
        g_attn_out=g_attn_out[l].reshape(1, HEAD_DIM),
        w_out_gdn=w_out[l, :gw].astype(BF16),
        w_out_att=w_out[l, gw:].astype(BF16),
        g_mix=g_norm_mix[l].reshape(1, -1),
        g_ffn=g_norm_ffn[l].reshape(1, -1),
        w_gate=w_ffn_gate[l].astype(BF16),
        w_up=w_ffn_up[l].astype(BF16),
        w_dw=w_ffn_dw[l],
        b_dw=b_ffn_dw[l].reshape(1, -1),
        w_down=w_ffn_down[l].astype(BF16),
    )


def _pad_rows_front(a, rows):
    return jnp.pad(a, ((0, 0), (rows - a.shape[1], 0), (0, 0)))


def _prompt_layer(x, mod, lw, *, batch, seq):
    shift1, scale1, gate1, shift2, scale2, gate2 = (mod[:, i][:, None, :] for i in range(6))
    kw = dict(per_row=False, rows_per_seq=seq)
    gw = lw["w_out_gdn"].shape[0]
    dff = lw["w_gate"].shape[1]
    nheads = gw // HEAD_DIM
    qkvz, ab = _inproj_gdn(x, scale1, shift1, lw["g_mix"], lw["w_gdn"], lw["w_ab"], tm=1024, tn=1024, **kw)
    mqkv = _inproj_moba(x, scale1, shift1, lw["g_mix"], lw["w_moba"], lw["qk_gains"], tm=1024, tn=512, **kw)
    o_gdn, s_new, conv_new = _gdn(
        qkvz, ab, jnp.zeros((batch, SUBLANES, 3 * gw), F32),
        jnp.zeros((1, batch, nheads, HEAD_DIM, HEAD_DIM), F32), 0, lw["w_conv"], lw["a_log"], lw["dt_bias"], lw["g_gdn_out"],
        batch=batch, seq=seq, t_valid=seq, tc=256, chunk=GDN_CHUNK)
    o_att = _moba_prompt(mqkv, lw["g_attn_out"], batch=batch, seq=seq)
    x = _outproj(o_gdn, o_att, lw["w_out_gdn"], lw["w_out_att"], x, gate1, tm=1024, tn=1024, **kw)
    ffn_tm = 512
    x, tails = _ffn(x, scale2, shift2, gate2, lw["g_ffn"], lw["w_gate"], lw["w_up"], lw["w_dw"], lw["b_dw"],
                    lw["w_down"], jnp.zeros((batch, SUBLANES, dff), F32), unit=1, tm=ffn_tm, tn=512, **kw)
    tps = seq // ffn_tm
    return x, mqkv[1], mqkv[2], s_new, conv_new, tails[tps - 1::tps]


def _sample_layer(x, mod_rows, lw, cache_k, cache_v, layer, page_table, s0, gdn_conv, ffn_conv, *, db, nq):
    shift1, scale1, gate1, shift2, scale2, gate2 = mod_rows
    m = nq * db
    kw = dict(per_row=True, rows_per_seq=m)
    gw = lw["w_out_gdn"].shape[0]
    dff = lw["w_gate"].shape[1]
    to_bm = lambda a: a.reshape(nq, db, -1).transpose(1, 0, 2)
    to_tm = lambda a: a.transpose(1, 0, 2).reshape(m, -1)
    qkvz, ab = _inproj_gdn(x, scale1, shift1, lw["g_mix"], lw["w_gdn"], lw["w_ab"], tm=m, tn=1024, **kw)
    mqkv = _inproj_moba(x, scale1, shift1, lw["g_mix"], lw["w_moba"], lw["qk_gains"], tm=m, tn=512, **kw)
    t_pad = SUBLANES
    pad_t = lambda a: jnp.pad(to_bm(a), ((0, 0), (0, t_pad - nq), (0, 0))).reshape(db * t_pad, -1)
    o_gdn, s_new, conv_new = _gdn(
        pad_t(qkvz), pad_t(ab), _pad_rows_front(gdn_conv, SUBLANES), s0, layer,
        lw["w_conv"], lw["a_log"], lw["dt_bias"], lw["g_gdn_out"],
        batch=db, seq=t_pad, t_valid=nq, tc=t_pad, chunk=t_pad)
    o_gdn = to_tm(o_gdn.reshape(db, t_pad, gw)[:, :nq])
    nheads_m = cache_k.shape[3]
    heads = lambda a: to_bm(a).reshape(db, nq, nheads_m, HEAD_DIM)
    q_bm, k_bm, v_bm = heads(mqkv[0]), heads(mqkv[1]), heads(mqkv[2])
    o_att = _moba_sample(q_bm, k_bm, v_bm, cache_k, cache_v, layer, page_table, lw["g_attn_out"])
    o_att = to_tm(o_att.reshape(db, nq, nheads_m * HEAD_DIM)).astype(BF16)
    x = _outproj(o_gdn, o_att, lw["w_out_gdn"], lw["w_out_att"], x, gate1, tm=m, tn=1024, **kw)
    state_tm = ffn_conv.transpose(1, 0, 2).reshape(1, (FFN_CONV - 1) * db, dff)
    x, ffn_new = _ffn(x, scale2, shift2, gate2, lw["g_ffn"], lw["w_gate"], lw["w_up"], lw["w_dw"], lw["b_dw"],
                      lw["w_down"], state_tm, unit=db, tm=m, tn=512, **kw)
    ffn_new = ffn_new.reshape(FFN_CONV - 1, db, dff).transpose(1, 0, 2)
    return x, k_bm, v_bm, s_new, conv_new, ffn_new


def kernel(x_prompt, x_sample, cache_k, cache_v, state_gdn, state_gdn_conv, state_ffn_conv, page_table, c_prompt, c_sample, w_ada, b_ada, g_norm_mix, w_in, w_gdn_conv, gdn_a_log, gdn_dt_bias, g_gdn_out, g_q, g_k, g_attn_out, w_out, g_norm_ffn, w_ffn_gate, w_ffn_up, w_ffn_dw, b_ffn_dw, w_ffn_down):
    depth = w_in.shape[0]
    batch, seq, d = x_prompt.shape
    db, nq, _ = x_sample.shape
    nheads_m = cache_k.shape[3]

    xp = x_prompt.reshape(batch * seq, d)
    xs = x_sample.transpose(1, 0, 2).reshape(nq * db, d)
    c_all = jnp.concatenate([c_prompt, c_sample], axis=0)

    outs_p = [[] for _ in range(5)]
    outs_s = [[] for _ in range(5)]
    for l in range(depth):
        lw = _layer_weights(l, w_in, w_gdn_conv, gdn_a_log, gdn_dt_bias, g_gdn_out, g_q, g_k, g_attn_out, w_out,
                            g_norm_mix, g_norm_ffn, w_ffn_gate, w_ffn_up, w_ffn_dw, b_ffn_dw, w_ffn_down)
        mod = _ada(c_all, w_ada, b_ada, l).reshape(batch + db, 6, d)
        mod_s = tuple(jnp.tile(mod[batch:, i], (nq, 1)) for i in range(6))

        xp, k_p, v_p, s_p, gc_p, fc_p = _prompt_layer(xp, mod[:batch], lw, batch=batch, seq=seq)
        for acc, val in zip(outs_p, (k_p.reshape(batch, seq, nheads_m, HEAD_DIM),
                                     v_p.reshape(batch, seq, nheads_m, HEAD_DIM), s_p, gc_p, fc_p)):
            acc.append(val)

        xs, k_s, v_s, s_s, gc_s, fc_s = _sample_layer(
            xs, mod_s, lw, cache_k, cache_v, l, page_table, state_gdn, state_gdn_conv[l], state_ffn_conv[l],
            db=db, nq=nq)
        for acc, val in zip(outs_s, (k_s, v_s, s_s, gc_s, fc_s)):
            acc.append(val)

    y_prompt = xp.reshape(batch, seq, d)
    y_sample = xs.reshape(nq, db, d).transpose(1, 0, 2)
    return (y_prompt, y_sample, *(jnp.stack(a) for a in outs_p), *(jnp.stack(a) for a in outs_s))
```

```python
import functools

import jax
import jax.numpy as jnp
from jax import lax
from jax.experimental import pallas as pl
from jax.experimental.pallas import tpu as pltpu

F32 = jnp.float32
BF16 = jnp.bfloat16

NORM_EPS = 1e-6
HEAD_DIM = 128
GDN_CONV = 4
GDN_CHUNK = 64
MOBA_BLOCK = 256
MOBA_TOPK = 3
FFN_CONV = 3
SUBLANES = 8
MXU_WIDTH = 256
VMEM_LIMIT = 56 * 1024 * 1024

_NT = (((1,), (1,)), ((), ()))
_NN = (((1,), (0,)), ((), ()))
_TN = (((0,), (0,)), ((), ()))


def _params(sem):
    return pltpu.CompilerParams(dimension_semantics=sem, vmem_limit_bytes=VMEM_LIMIT)


def _sigmoid(x):
    return 1.0 / (1.0 + jnp.exp(-x))


def _silu(x):
    return x * _sigmoid(x)


def _dot(a, b, dims=_NN):
    return lax.dot_general(a.astype(BF16), b.astype(BF16), dims, preferred_element_type=F32)


def _split(a):
    hi = a.astype(BF16)
    lo = (a - hi.astype(F32)).astype(BF16)
    return hi, lo


def _dot3(a, b, dims=_NN):
    ah, al = _split(a)
    bh, bl = _split(b)
    d = lambda x, y: lax.dot_general(x, y, dims, preferred_element_type=F32)
    return d(ah, bh) + (d(ah, bl) + d(al, bh))


def _norm_mod(x, g, scale, shift):
    ms = jnp.mean(x * x, axis=-1, keepdims=True)
    y = x * lax.rsqrt(ms + NORM_EPS) * g
    return y * (1.0 + scale) + shift


def _head_rmsnorm(x, g):
    ms = jnp.mean(x * x, axis=-1, keepdims=True)
    return x * lax.rsqrt(ms + NORM_EPS) * g


def _ada_kernel(c_ref, w_ref, b_ref, o_ref):
    s = _silu(c_ref[...])
    o_ref[...] = _dot(s, w_ref[0]) + b_ref[0]


def _ada(c, w, b, layer, tn=1024):
    r, d = c.shape
    n = w.shape[2]
    return pl.pallas_call(
        _ada_kernel,
        grid=(n // tn,),
        in_specs=[pl.BlockSpec((r, d), lambda j: (0, 0)),
                  pl.BlockSpec((1, d, tn), lambda j: (layer, 0, j)),
                  pl.BlockSpec((1, 1, tn), lambda j: (layer, 0, j))],
        out_specs=pl.BlockSpec((r, tn), lambda j: (0, j)),
        out_shape=jax.ShapeDtypeStruct((r, n), F32),
        compiler_params=_params(("arbitrary",)),
        name="ada",
    )(c, w, b.reshape(b.shape[0], 1, n))


def _mod_spec(per_row, tm, width, rows_per_seq, col_of=None):
    if col_of is None:
        col_of = lambda j: 0
    if per_row:
        return pl.BlockSpec((tm, width), lambda i, j: (i, col_of(j)))
    tps = rows_per_seq // tm
    return pl.BlockSpec((1, 1, width), lambda i, j: (i // tps, 0, col_of(j)))


def _mod_val(ref, per_row):
    return ref[...] if per_row else ref[0]


def _inproj_gdn_kernel(x_ref, sc_ref, sh_ref, g_ref, w_ref, wab_ref, o_ref, oab_ref, h_scr, *, per_row):
    @pl.when(pl.program_id(1) == 0)
    def _():
        h = _norm_mod(x_ref[...], g_ref[...], _mod_val(sc_ref, per_row), _mod_val(sh_ref, per_row))
        hb = h.astype(BF16)
        h_scr[...] = hb
        oab_ref[...] = jnp.dot(hb, wab_ref[...], preferred_element_type=F32)

    o_ref[...] = jnp.dot(h_scr[...], w_ref[...], preferred_element_type=F32)


def _inproj_gdn(x, scale, shift, g, w, wab, *, per_row, rows_per_seq, tm, tn):
    m, d = x.shape
    n = w.shape[1]
    nab = wab.shape[1]
    mod = _mod_spec(per_row, tm, d, rows_per_seq)
    return pl.pallas_call(
        functools.partial(_inproj_gdn_kernel, per_row=per_row),
        grid=(m // tm, n // tn),
        in_specs=[pl.BlockSpec((tm, d), lambda i, j: (i, 0)), mod, mod,
                  pl.BlockSpec((1, d), lambda i, j: (0, 0)),
                  pl.BlockSpec((d, tn), lambda i, j: (0, j)),
                  pl.BlockSpec((d, nab), lambda i, j: (0, 0))],
        out_specs=[pl.BlockSpec((tm, tn), lambda i, j: (i, j)),
                   pl.BlockSpec((tm, nab), lambda i, j: (i, 0))],
        out_shape=[jax.ShapeDtypeStruct((m, n), F32), jax.ShapeDtypeStruct((m, nab), F32)],
        scratch_shapes=[pltpu.VMEM((tm, d), BF16)],
        compiler_params=_params(("arbitrary", "arbitrary")),
        name="inproj_gdn",
    )(x, scale, shift, g, w, wab)


def _inproj_moba_kernel(x_ref, sc_ref, sh_ref, g_ref, w_ref, gains_ref, o_ref, h_scr, *, per_row, tiles_per_sec):
    j = pl.program_id(1)

    @pl.when(j == 0)
    def _():
        h = _norm_mod(x_ref[...], g_ref[...], _mod_val(sc_ref, per_row), _mod_val(sh_ref, per_row))
        h_scr[...] = h.astype(BF16)

    r = jnp.dot(h_scr[...], w_ref[...], preferred_element_type=F32)
    tn = r.shape[1]

    @pl.when(j < 2 * tiles_per_sec)
    def _():
        gain = gains_ref[0]
        for c in range(tn // HEAD_DIM):
            cols = slice(c * HEAD_DIM, (c + 1) * HEAD_DIM)
            o_ref[0, :, cols] = _head_rmsnorm(r[:, cols], gain)

    @pl.when(j >= 2 * tiles_per_sec)
    def _():
        o_ref[0] = r


def _inproj_moba(x, scale, shift, g, w, gains, *, per_row, rows_per_seq, tm, tn):
    m, d = x.shape
    width = w.shape[1] // 3
    tps = width // tn
    mod = _mod_spec(per_row, tm, d, rows_per_seq)
    return pl.pallas_call(
        functools.partial(_inproj_moba_kernel, per_row=per_row, tiles_per_sec=tps),
        grid=(m // tm, 3 * tps),
        in_specs=[pl.BlockSpec((tm, d), lambda i, j: (i, 0)), mod, mod,
                  pl.BlockSpec((1, d), lambda i, j: (0, 0)),
                  pl.BlockSpec((d, tn), lambda i, j: (0, j)),
                  pl.BlockSpec((1, 1, HEAD_DIM), lambda i, j: (jnp.minimum(j // tps, 1), 0, 0))],
        out_specs=pl.BlockSpec((1, tm, tn), lambda i, j: (j // tps, i, j % tps)),
        out_shape=jax.ShapeDtypeStruct((3, m, width), F32),
        scratch_shapes=[pltpu.VMEM((tm, d), BF16)],
        compiler_params=_params(("arbitrary", "arbitrary")),
        name="inproj_moba",
    )(x, scale, shift, g, w, gains)


def _gdn_kernel(qkv_ref, z_ref, ab_ref, cst_ref, s0_ref, wconv_ref, alog_ref, dtb_ref, gout_ref,
                o_ref, sfin_ref, cout_ref,
                s_scr, xe_scr, y_scr, g_scr, beta_scr, *, chunk, t_valid, nheads):
    t = pl.program_id(1)
    nt = pl.num_programs(1)
    tc = qkv_ref.shape[0]
    width = nheads * HEAD_DIM
    halo = SUBLANES

    @pl.when(t == 0)
    def _():
        s_scr[...] = s0_ref[0, 0]
        xe_scr[0:halo] = cst_ref[0]

    x = qkv_ref[...]
    xe_scr[halo:halo + tc] = x
    w = wconv_ref[...]
    y = w[GDN_CONV - 1:GDN_CONV] * x
    for tap in range(GDN_CONV - 1):
        off = halo - (GDN_CONV - 1) + tap
        y = y + w[tap:tap + 1] * xe_scr[off:off + tc]
    y_scr[...] = _silu(y)

    ab = ab_ref[...]
    beta = _sigmoid(ab)
    sp_in = ab + dtb_ref[...]
    softplus = jnp.maximum(sp_in, 0.0) + jnp.log(1.0 + jnp.exp(-jnp.abs(sp_in)))
    g = -jnp.exp(alog_ref[...]) * softplus
    row = lax.broadcasted_iota(jnp.int32, (tc, HEAD_DIM), 0)
    if t_valid % tc != 0:
        valid = (t * tc + row) < t_valid
        beta = jnp.where(valid, beta, 0.0)
        g = jnp.where(valid, g, 0.0)
    in_chunk = row % chunk
    step = 1
    while step < chunk:
        g = g + jnp.where(in_chunk >= step, pltpu.roll(g, step, axis=0), 0.0)
        step *= 2
    g_scr[...] = g
    beta_scr[...] = beta

    ri = lax.broadcasted_iota(jnp.int32, (chunk, chunk), 0)
    ci = lax.broadcasted_iota(jnp.int32, (chunk, chunk), 1)
    causal = ri >= ci
    strict = ri > ci
    eye = jnp.where(ri == ci, 1.0, 0.0).astype(F32)
    gout = gout_ref[...]
    n_square = max(chunk.bit_length() - 2, 0)

    def chunk_body(c, carry):
        rows = pl.ds(pl.multiple_of(c * chunk, chunk), chunk)
        gc = g_scr[rows, :]
        gct = gc.T
        bc = beta_scr[rows, :]
        heads = range(nheads)
        cols = [slice(h * HEAD_DIM, (h + 1) * HEAD_DIM) for h in heads]
        q = [y_scr[rows, cols[h]] for h in heads]
        k = [y_scr[rows, width + h * HEAD_DIM:width + (h + 1) * HEAD_DIM] for h in heads]
        v = [y_scr[rows, 2 * width + h * HEAD_DIM:2 * width + (h + 1) * HEAD_DIM] for h in heads]
        q = [a * lax.rsqrt(jnp.sum(a * a, axis=-1, keepdims=True) + NORM_EPS) * (HEAD_DIM ** -0.5) for a in q]
        k = [a * lax.rsqrt(jnp.sum(a * a, axis=-1, keepdims=True) + NORM_EPS) for a in k]
        g_col = [gc[:, h:h + 1] for h in heads]
        b_col = [bc[:, nheads + h:nheads + h + 1] for h in heads]
        g_end = [gc[chunk - 1:chunk, h:h + 1] for h in heads]
        decay = [jnp.where(causal, jnp.exp(g_col[h] - gct[h:h + 1, :]), 0.0) for h in heads]
        kk = [_dot3(k[h], k[h], _NT) for h in heads]
        qk = [_dot(q[h], k[h], _NT) * decay[h] for h in heads]
        pw = [jnp.where(strict, -(b_col[h] * kk[h] * decay[h]), 0.0) for h in heads]
        inv = [eye + pw[h] for h in heads]
        for _ in range(n_square):
            pw = [_dot3(pw[h], pw[h]) for h in heads]
            inv = [inv[h] + _dot3(inv[h], pw[h]) for h in heads]
        eg = [jnp.exp(g_col[h]) for h in heads]
        sol = [_dot3(inv[h], jnp.concatenate([b_col[h] * v[h], (b_col[h] * eg[h]) * k[h]], axis=-1)) for h in heads]
        s_old = [s_scr[h] for h in heads]
        ws = [_dot(jnp.concatenate([sol[h][:, HEAD_DIM:], q[h] * eg[h]], axis=0), s_old[h]) for h in heads]
        v_new = [sol[h][:, :HEAD_DIM] - ws[h][:chunk] for h in heads]
        o = [ws[h][chunk:] + _dot(qk[h], v_new[h]) for h in heads]
        s_new = [s_old[h] * jnp.exp(g_end[h]) + _dot(k[h] * jnp.exp(g_end[h] - g_col[h]), v_new[h], _TN)
                 for h in heads]
        for h in heads:
            s_scr[h] = s_new[h]
            gated = _head_rmsnorm(o[h], gout) * _silu(z_ref[rows, cols[h]])
            o_ref[rows, cols[h]] = gated.astype(o_ref.dtype)
        return carry

    lax.fori_loop(0, tc // chunk, chunk_body, 0)

    xe_scr[0:halo] = xe_scr[tc:tc + halo]

    @pl.when(t == nt - 1)
    def _():
        sfin_ref[0] = s_scr[...]
        end = halo + t_valid - (t_valid - 1) // tc * tc
        cout_ref[0] = xe_scr[end - (GDN_CONV - 1):end]


def _gdn(qkvz, ab, conv_state8, s0, layer, wconv, alog, dtb, gout, *, batch, seq, t_valid, tc, chunk):
    nheads = s0.shape[2]
    width = nheads * HEAD_DIM
    nt = seq // tc
    assert t_valid >= GDN_CONV - 1 and tc >= SUBLANES and (t_valid - 1) // tc == nt - 1
    return pl.pallas_call(
        functools.partial(_gdn_kernel, chunk=chunk, t_valid=t_valid, nheads=nheads),
        grid=(batch, nt),
        in_specs=[pl.BlockSpec((tc, 3 * width), lambda b, t: (b * nt + t, 0)),
                  pl.BlockSpec((tc, width), lambda b, t: (b * nt + t, 3)),
                  pl.BlockSpec((tc, HEAD_DIM), lambda b, t: (b * nt + t, 0)),
                  pl.BlockSpec((1, SUBLANES, 3 * width), lambda b, t: (b, 0, 0)),
                  pl.BlockSpec((1, 1, nheads, HEAD_DIM, HEAD_DIM), lambda b, t: (layer, b, 0, 0, 0)),
                  pl.BlockSpec((GDN_CONV, 3 * width), lambda b, t: (0, 0)),
                  pl.BlockSpec((1, HEAD_DIM), lambda b, t: (0, 0)),
                  pl.BlockSpec((1, HEAD_DIM), lambda b, t: (0, 0)),
                  pl.BlockSpec((1, HEAD_DIM), lambda b, t: (0, 0))],
        out_specs=[pl.BlockSpec((tc, width), lambda b, t: (b * nt + t, 0)),
                   pl.BlockSpec((1, nheads, HEAD_DIM, HEAD_DIM), lambda b, t: (b, 0, 0, 0)),
                   pl.BlockSpec((1, GDN_CONV - 1, 3 * width), lambda b, t: (b, 0, 0))],
        out_shape=[jax.ShapeDtypeStruct((batch * seq, width), BF16),
                   jax.ShapeDtypeStruct(s0.shape[1:], F32),
                   jax.ShapeDtypeStruct((batch, GDN_CONV - 1, 3 * width), F32)],
        scratch_shapes=[pltpu.VMEM((nheads, HEAD_DIM, HEAD_DIM), F32),
                        pltpu.VMEM((tc + SUBLANES, 3 * width), F32),
                        pltpu.VMEM((tc, 3 * width), F32),
                        pltpu.VMEM((tc, HEAD_DIM), F32),
                        pltpu.VMEM((tc, HEAD_DIM), F32)],
        compiler_params=_params(("arbitrary", "arbitrary")),
        name="gdn",
    )(qkvz, qkvz, ab, conv_state8, s0, wconv, alog, dtb, gout)


def _select_blocks(gate, n_past, axis):
    nb = gate.shape[axis]
    idx = lax.broadcasted_iota(jnp.int32, gate.shape, axis)
    rank = jnp.zeros(gate.shape, F32)
    for m in range(nb):
        gm = gate[m:m + 1, :] if axis == 0 else gate[:, m:m + 1]
        beats = (gm > gate) | ((gm == gate) & (m < idx))
        rank = rank + jnp.where(beats & (m < n_past), 1.0, 0.0)
    return jnp.where((idx < n_past) & (rank < MOBA_TOPK), 1.0, 0.0)


def _moba_prompt_kernel(q_ref, k_ref, v_ref, gout_ref, o_ref, km_scr, vt_scr, *, group):
    j = pl.program_id(2)
    blk = MOBA_BLOCK
    nb = k_ref.shape[1] // blk
    scale = HEAD_DIM ** -0.5
    heads = range(group)
    cols = [slice(g * HEAD_DIM, (g + 1) * HEAD_DIM) for g in heads]

    @pl.when(j == 0)
    def _():
        for g in heads:
            for n in range(nb):
                rows = slice(n * blk, (n + 1) * blk)
                km_scr[g, n:n + 1, :] = jnp.mean(k_ref[0, rows, cols[g]], axis=0, keepdims=True)
                vt_scr[g, :, rows] = v_ref[0, rows, cols[g]].T.astype(BF16)

    q = [q_ref[0, :, cols[g]] for g in heads]
    gate_t = [_dot3(km_scr[g], q[g], _NT) for g in heads]
    sel_bias = [jnp.where(_select_blocks(gate_t[g], j, axis=0) > 0.0, 0.0, -jnp.inf) for g in heads]
    qb = [x.astype(BF16) for x in q]
    ki = lax.broadcasted_iota(jnp.int32, (blk, blk), 0)
    qi = lax.broadcasted_iota(jnp.int32, (blk, blk), 1)
    tri_bias = jnp.where(ki <= qi, 0.0, -jnp.inf)
    gout = gout_ref[...]

    def attend(own):
        s = [[] for _ in heads]
        for n in range(own + 1):
            rows = slice(n * blk, (n + 1) * blk)
            for g in heads:
                sn = lax.dot_general(k_ref[0, rows, cols[g]].astype(BF16), qb[g], _NT,
                                     preferred_element_type=F32) * scale
                s[g].append(sn + (tri_bias if n == own else sel_bias[g][n:n + 1, :]))
        m = [jnp.max(s[g][own], axis=0, keepdims=True) for g in heads]
        for n in range(own):
            m = [jnp.maximum(m[g], jnp.max(s[g][n], axis=0, keepdims=True)) for g in heads]
        l = [jnp.zeros_like(m[g]) for g in heads]
        acc = [jnp.zeros((HEAD_DIM, blk), F32) for _ in heads]
        for n in range(own + 1):
            for g in heads:
                p = jnp.exp(s[g][n] - m[g])
                l[g] = l[g] + jnp.sum(p, axis=0, keepdims=True)
                acc[g] = acc[g] + jnp.dot(vt_scr[g, :, n * blk:(n + 1) * blk], p.astype(BF16),
                                          preferred_element_type=F32)
        for g in heads:
            o = (acc[g] / l[g]).T
            o_ref[:, cols[g]] = _head_rmsnorm(o, gout).astype(o_ref.dtype)

    for own in range(nb):
        pl.when(j == own)(functools.partial(attend, own))


def _moba_prompt(qkv, gout, *, batch, seq, group=4):
    width = qkv.shape[2]
    nheads = width // HEAD_DIM
    nb = seq // MOBA_BLOCK
    assert seq % MOBA_BLOCK == 0 and nheads % group == 0
    gw = group * HEAD_DIM
    return pl.pallas_call(
        functools.partial(_moba_prompt_kernel, group=group),
        grid=(batch, nheads // group, nb),
        in_specs=[pl.BlockSpec((1, MOBA_BLOCK, gw), lambda b, h, j: (0, b * nb + j, h)),
                  pl.BlockSpec((1, seq, gw), lambda b, h, j: (1, b, h)),
                  pl.BlockSpec((1, seq, gw), lambda b, h, j: (2, b, h)),
                  pl.BlockSpec((1, HEAD_DIM), lambda b, h, j: (0, 0))],
        out_specs=pl.BlockSpec((MOBA_BLOCK, gw), lambda b, h, j: (b * nb + j, h)),
        out_shape=jax.ShapeDtypeStruct((batch * seq, width), BF16),
        scratch_shapes=[pltpu.VMEM((group, max(nb, SUBLANES), HEAD_DIM), F32),
                        pltpu.VMEM((group, HEAD_DIM, seq), BF16)],
        compiler_params=_params(("arbitrary", "arbitrary", "arbitrary")),
        name="moba_prompt",
    )(qkv, qkv, qkv, gout)


def _moba_sample_kernel(pt_ref, q_ref, kn_ref, vn_ref, *refs, nheads, nq, bps):
    del pt_ref
    k_refs, v_refs = refs[:2 * bps], refs[2 * bps:4 * bps]
    gout_ref, o_ref, bias_scr, m_scr, l_scr, gate_scr, acc_scr = refs[4 * bps:]
    n = pl.program_id(1)
    nb = pl.num_programs(1)
    nr = nq * nheads
    scale = HEAD_DIM ** -0.5
    page = k_refs[0].shape[2]
    nkeys = 2 * page * nheads

    @pl.when(n == 0)
    def _():
        col_head = lax.broadcasted_iota(jnp.int32, (nr, nkeys), 1) % nheads
        row_head = lax.broadcasted_iota(jnp.int32, (nr, nkeys), 0) % nheads
        bias_scr[...] = jnp.where(col_head == row_head, 0.0, -jnp.inf)

    q3 = q_ref[0]
    qb = q3.reshape(nr, HEAD_DIM).astype(BF16)
    flat = lambda a: a.reshape(page * nheads, HEAD_DIM).astype(BF16)
    lane = lax.broadcasted_iota(jnp.int32, (nr, HEAD_DIM), 1)
    blocks = range(bps)
    k3 = [(k_refs[2 * i][0, 0], k_refs[2 * i + 1][0, 0]) for i in blocks]
    k_mean = [(jnp.sum(a, axis=0) + jnp.sum(b, axis=0)) / MOBA_BLOCK for a, b in k3]
    gate = [jnp.concatenate([jnp.sum(q3[i] * km, axis=-1, keepdims=True) for i in range(nq)], axis=0)
            for km in k_mean]
    s = [lax.dot_general(qb, jnp.concatenate([flat(a), flat(b)], axis=0), _NT, preferred_element_type=F32) * scale
         + bias_scr[...] for a, b in k3]
    m = [jnp.max(x, axis=-1, keepdims=True) for x in s]
    p = [jnp.exp(x - mx) for x, mx in zip(s, m)]
    l = [jnp.sum(x, axis=-1, keepdims=True) for x in p]
    v2 = [jnp.concatenate([flat(v_refs[2 * i][0, 0]), flat(v_refs[2 * i + 1][0, 0])], axis=0) for i in blocks]
    pv = [jnp.dot(x.astype(BF16), v, preferred_element_type=F32) for x, v in zip(p, v2)]
    first = n == 0
    m_all, l_all, g_all = (jnp.where(first, 0.0, r[...]) for r in (m_scr, l_scr, gate_scr))
    for i in blocks:
        here = lane == n * bps + i
        m_all = jnp.where(here, m[i], m_all)
        l_all = jnp.where(here, l[i], l_all)
        g_all = jnp.where(here, gate[i], g_all)
        acc_scr[n * bps + i] = pv[i]
    m_scr[...] = m_all
    l_scr[...] = l_all
    gate_scr[...] = g_all

    @pl.when(n == nb - 1)
    def _():
        n_past = acc_scr.shape[0]
        sel = _select_blocks(gate_scr[:, 0:n_past], n_past, axis=1) > 0.0
        kn2 = kn_ref[0].reshape(nr, HEAD_DIM)
        vn2 = vn_ref[0].reshape(nr, HEAD_DIM)
        s_own = lax.dot_general(qb, kn2.astype(BF16), _NT, preferred_element_type=F32) * scale
        ri = lax.broadcasted_iota(jnp.int32, (nr, nr), 0)
        ci = lax.broadcasted_iota(jnp.int32, (nr, nr), 1)
        own_ok = (ci % nheads == ri % nheads) & (ci // nheads <= ri // nheads)
        s_own = jnp.where(own_ok, s_own, -jnp.inf)
        m_own = jnp.max(s_own, axis=-1, keepdims=True)
        m_all = jnp.where(sel, m_scr[:, 0:n_past], -jnp.inf)
        m_fin = jnp.maximum(m_own, jnp.max(m_all, axis=-1, keepdims=True))
        p_own = jnp.exp(s_own - m_fin)
        wgt = jnp.where(sel, jnp.exp(m_all - m_fin), 0.0)
        l_fin = (jnp.sum(p_own, axis=-1, keepdims=True)
                 + jnp.sum(wgt * l_scr[:, 0:n_past], axis=-1, keepdims=True))
        o = _dot(p_own, vn2)
        for b in range(n_past):
            o = o + wgt[:, b:b + 1] * acc_scr[b]
        o = _head_rmsnorm(o / l_fin, gout_ref[...])
        o_ref[0] = o.reshape(nq, nheads, HEAD_DIM)


def _moba_sample(q, k_new, v_new, cache_k, cache_v, layer, page_table, gout):
    db, nq, nheads, _ = q.shape
    page = cache_k.shape[2]
    n_pages = page_table.shape[1]
    assert MOBA_BLOCK == 2 * page and n_pages % 2 == 0
    nb = n_pages // 2
    bps = 2 if nb % 2 == 0 else 1
    nr = nq * nheads
    assert nb <= HEAD_DIM and nheads == SUBLANES
    pt = page_table.reshape(-1)

    def page_spec(k):
        return pl.BlockSpec((1, 1, page, nheads, HEAD_DIM),
                            lambda b, n, pt_ref: (layer, pt_ref[b * n_pages + 2 * bps * n + k], 0, 0, 0))

    pages = [page_spec(k) for k in range(2 * bps)]
    tok = pl.BlockSpec((1, nq, nheads, HEAD_DIM), lambda b, n, pt_ref: (b, 0, 0, 0))
    grid_spec = pltpu.PrefetchScalarGridSpec(
        num_scalar_prefetch=1,
        grid=(db, nb // bps),
        in_specs=[tok, tok, tok, *pages, *pages,
                  pl.BlockSpec((1, HEAD_DIM), lambda b, n, pt_ref: (0, 0))],
        out_specs=tok,
        scratch_shapes=[pltpu.VMEM((nr, 2 * page * nheads), F32),
                        pltpu.VMEM((nr, HEAD_DIM), F32),
                        pltpu.VMEM((nr, HEAD_DIM), F32),
                        pltpu.VMEM((nr, HEAD_DIM), F32),
                        pltpu.VMEM((nb, nr, HEAD_DIM), F32)])
    return pl.pallas_call(
        functools.partial(_moba_sample_kernel, nheads=nheads, nq=nq, bps=bps),
        grid_spec=grid_spec,
        out_shape=jax.ShapeDtypeStruct((db, nq, nheads, HEAD_DIM), F32),
        compiler_params=_params(("arbitrary", "arbitrary")),
        name="moba_sample",
    )(pt, q, k_new, v_new, *([cache_k] * (2 * bps)), *([cache_v] * (2 * bps)), gout)


def _outproj_kernel(og_ref, oa_ref, wg_ref, wa_ref, x_ref, gt_ref, y_ref, *, per_row):
    mix = (jnp.dot(og_ref[...], wg_ref[...], preferred_element_type=F32)
           + jnp.dot(oa_ref[...], wa_ref[...], preferred_element_type=F32))
    y_ref[...] = x_ref[...] + _mod_val(gt_ref, per_row) * mix


def _outproj(og, oa, wg, wa, x, gate, *, per_row, rows_per_seq, tm, tn):
    m, d = x.shape
    kw = og.shape[1]
    return pl.pallas_call(
        functools.partial(_outproj_kernel, per_row=per_row),
        grid=(m // tm, d // tn),
        in_specs=[pl.BlockSpec((tm, kw), lambda i, j: (i, 0)),
                  pl.BlockSpec((tm, kw), lambda i, j: (i, 0)),
                  pl.BlockSpec((kw, tn), lambda i, j: (0, j)),
                  pl.BlockSpec((kw, tn), lambda i, j: (0, j)),
                  pl.BlockSpec((tm, tn), lambda i, j: (i, j)),
                  _mod_spec(per_row, tm, tn, rows_per_seq, col_of=lambda j: j)],
        out_specs=pl.BlockSpec((tm, tn), lambda i, j: (i, j)),
        out_shape=jax.ShapeDtypeStruct((m, d), F32),
        compiler_params=_params(("arbitrary", "arbitrary")),
        name="outproj",
    )(og, oa, wg, wa, x, gate)


def _ffn_kernel(x_ref, sc_ref, sh_ref, gt_ref, g_ref, wg_ref, wu_ref, wdw_ref, bdw_ref, wd_ref, st_ref,
                y_ref, cs_ref, h_scr, acc_scr, prev_scr, ext_scr, *, per_row, tiles_per_seq, unit, halo):
    i = pl.program_id(0)
    j = pl.program_id(1)
    tm = x_ref.shape[0]
    keep = (FFN_CONV - 1) * unit

    @pl.when(j == 0)
    def _():
        h = _norm_mod(x_ref[...], g_ref[...], _mod_val(sc_ref, per_row), _mod_val(sh_ref, per_row))
        h_scr[...] = h.astype(BF16)
        acc_scr[...] = jnp.zeros_like(acc_scr)

    hb = h_scr[...]
    first = (i % tiles_per_seq) == 0

    @pl.when(first)
    def _():
        ext_scr[0:halo] = st_ref[0]

    @pl.when(jnp.logical_not(first))
    def _():
        ext_scr[0:halo] = prev_scr[j]

    tn = wg_ref.shape[1]
    n_split = 2 if tn % (2 * MXU_WIDTH) == 0 else 1
    cw = tn // n_split
    parts = [slice(s * cw, (s + 1) * cw) for s in range(n_split)]
    w = wdw_ref[...]
    bias = bdw_ref[...]
    gps = [jnp.dot(hb, wg_ref[:, c], preferred_element_type=F32) for c in parts]
    ups = [jnp.dot(hb, wu_ref[:, c], preferred_element_type=F32) for c in parts]
    acts = []
    for c, gp, up in zip(parts, gps, ups):
        ext_scr[halo:halo + tm, c] = gp
        gc = w[FFN_CONV - 1:FFN_CONV, c] * gp + bias[:, c]
        for tap in range(FFN_CONV - 1):
            off = halo - (FFN_CONV - 1 - tap) * unit
            gc = gc + w[tap:tap + 1, c] * ext_scr[off:off + tm, c]
        acts.append((_silu(gc) * up).astype(BF16))
    prev_scr[j] = ext_scr[tm:tm + halo]
    cs_ref[0] = ext_scr[halo + tm - keep:halo + tm]
    down = jnp.dot(acts[0], wd_ref[parts[0], :], preferred_element_type=F32)
    for a, c in zip(acts[1:], parts[1:]):
        down = down + jnp.dot(a, wd_ref[c, :], preferred_element_type=F32)
    acc_scr[...] += down

    @pl.when(j == pl.num_programs(1) - 1)
    def _():
        y_ref[...] = x_ref[...] + _mod_val(gt_ref, per_row) * acc_scr[...]


def _ffn(x, scale, shift, gate, g, wg, wu, wdw, bdw, wd, state, *, per_row, rows_per_seq, unit, tm, tn):
    m, d = x.shape
    dff = wg.shape[1]
    nj = dff // tn
    halo = state.shape[1]
    keep = (FFN_CONV - 1) * unit
    tps = rows_per_seq // tm
    mod = _mod_spec(per_row, tm, d, rows_per_seq)
    return pl.pallas_call(
        functools.partial(_ffn_kernel, per_row=per_row, tiles_per_seq=tps, unit=unit, halo=halo),
        grid=(m // tm, nj),
        in_specs=[pl.BlockSpec((tm, d), lambda i, j: (i, 0)), mod, mod, mod,
                  pl.BlockSpec((1, d), lambda i, j: (0, 0)),
                  pl.BlockSpec((d, tn), lambda i, j: (0, j)),
                  pl.BlockSpec((d, tn), lambda i, j: (0, j)),
                  pl.BlockSpec((FFN_CONV, tn), lambda i, j: (0, j)),
                  pl.BlockSpec((1, tn), lambda i, j: (0, j)),
                  pl.BlockSpec((tn, d), lambda i, j: (j, 0)),
                  pl.BlockSpec((1, halo, tn), lambda i, j: (i // tps, 0, j))],
        out_specs=[pl.BlockSpec((tm, d), lambda i, j: (i, 0)),
                   pl.BlockSpec((1, keep, tn), lambda i, j: (i, 0, j))],
        out_shape=[jax.ShapeDtypeStruct((m, d), F32),
                   jax.ShapeDtypeStruct((m // tm, keep, dff), F32)],
        scratch_shapes=[pltpu.VMEM((tm, d), BF16),
                        pltpu.VMEM((tm, d), F32),
                        pltpu.VMEM((nj, halo, tn), F32),
                        pltpu.VMEM((tm + halo, tn), F32)],
        compiler_params=_params(("arbitrary", "arbitrary")),
        name="ffn",
    )(x, scale, shift, gate, g, wg, wu, wdw, bdw, wd, state)


def _layer_weights(l, w_in, w_gdn_conv, gdn_a_log, gdn_dt_bias, g_gdn_out, g_q, g_k, g_attn_out, w_out,
                   g_norm_mix, g_norm_ffn, w_ffn_gate, w_ffn_up, w_ffn_dw, b_ffn_dw, w_ffn_down):
    nheads = gdn_a_log.shape[1]
    gw = nheads * HEAD_DIM
    o_z, o_a = 3 * gw, 4 * gw
    o_mq = o_a + 2 * nheads
    wi = w_in[l]
    d = wi.shape[0]
    wab = jnp.zeros((d, HEAD_DIM), F32).at[:, :2 * nheads].set(wi[:, o_a:o_mq])
    lane_pad = lambda a: jnp.zeros((1, HEAD_DIM), F32).at[0, :nheads].set(a)
    return dict(
        w_gdn=wi[:, :o_a].astype(BF16),
        w_ab=wab.astype(BF16),
        w_moba=wi[:, o_mq:].astype(BF16),
        qk_gains=jnp.stack([g_q[l], g_k[l]]).reshape(2, 1, HEAD_DIM),
        w_conv=w_gdn_conv[l],
        a_log=lane_pad(gdn_a_log[l]),
        dt_bias=lane_pad(gdn_dt_bias[l]),
        g_gdn_out=g_gdn_out[l].reshape(1, HEAD_DIM),
        g_attn_out=g_attn_out[l].reshape(1, HEAD_DIM),
        w_out_gdn=w_out[l, :gw].astype(BF16),
        w_out_att=w_out[l, gw:].astype(BF16),
        g_mix=g_norm_mix[l].reshape(1, -1),
        g_ffn=g_norm_ffn[l].reshape(1, -1),
        w_gate=w_ffn_gate[l].astype(BF16),
        w_up=w_ffn_up[l].astype(BF16),
        w_dw=w_ffn_dw[l],
        b_dw=b_ffn_dw[l].reshape(1, -1),
        w_down=w_ffn_down[l].astype(BF16),
    )


def _pad_rows_front(a, rows):
    return jnp.pad(a, ((0, 0), (rows - a.shape[1], 0), (0, 0)))


def _prompt_layer(x, mod, lw, *, batch, seq):
    shift1, scale1, gate1, shift2, scale2, gate2 = (mod[:, i][:, None, :] for i in range(6))
    kw = dict(per_row=False, rows_per_seq=seq)
    gw = lw["w_out_gdn"].shape[0]
    dff = lw["w_gate"].shape[1]
    nheads = gw // HEAD_DIM
    qkvz, ab = _inproj_gdn(x, scale1, shift1, lw["g_mix"], lw["w_gdn"], lw["w_ab"], tm=1024, tn=1024, **kw)
    mqkv = _inproj_moba(x, scale1, shift1, lw["g_mix"], lw["w_moba"], lw["qk_gains"], tm=1024, tn=512, **kw)
    o_gdn, s_new, conv_new = _gdn(
        qkvz, ab, jnp.zeros((batch, SUBLANES, 3 * gw), F32),
        jnp.zeros((1, batch, nheads, HEAD_DIM, HEAD_DIM), F32), 0, lw["w_conv"], lw["a_log"], lw["dt_bias"], lw["g_gdn_out"],
        batch=batch, seq=seq, t_valid=seq, tc=256, chunk=GDN_CHUNK)
    o_att = _moba_prompt(mqkv, lw["g_attn_out"], batch=batch, seq=seq)
    x = _outproj(o_gdn, o_att, lw["w_out_gdn"], lw["w_out_att"], x, gate1, tm=1024, tn=1024, **kw)
    ffn_tm = 512
    x, tails = _ffn(x, scale2, shift2, gate2, lw["g_ffn"], lw["w_gate"], lw["w_up"], lw["w_dw"], lw["b_dw"],
                    lw["w_down"], jnp.zeros((batch, SUBLANES, dff), F32), unit=1, tm=ffn_tm, tn=512, **kw)
    tps = seq // ffn_tm
    return x, mqkv[1], mqkv[2], s_new, conv_new, tails[tps - 1::tps]


def _sample_layer(x, mod_rows, lw, cache_k, cache_v, layer, page_table, s0, gdn_conv, ffn_conv, *, db, nq):
    shift1, scale1, gate1, shift2, scale2, gate2 = mod_rows
    m = nq * db
    kw = dict(per_row=True, rows_per_seq=m)
    gw = lw["w_out_gdn"].shape[0]
    dff = lw["w_gate"].shape[1]
    to_bm = lambda a: a.reshape(nq, db, -1).transpose(1, 0, 2)
    to_tm = lambda a: a.transpose(1, 0, 2).reshape(m, -1)
    qkvz, ab = _inproj_gdn(x, scale1, shift1, lw["g_mix"], lw["w_gdn"], lw["w_ab"], tm=m, tn=1024, **kw)
    mqkv = _inproj_moba(x, scale1, shift1, lw["g_mix"], lw["w_moba"], lw["qk_gains"], tm=m, tn=512, **kw)
    t_pad = SUBLANES
    pad_t = lambda a: jnp.pad(to_bm(a), ((0, 0), (0, t_pad - nq), (0, 0))).reshape(db * t_pad, -1)
    o_gdn, s_new, conv_new = _gdn(
        pad_t(qkvz), pad_t(ab), _pad_rows_front(gdn_conv, SUBLANES), s0, layer,
        lw["w_conv"], lw["a_log"], lw["dt_bias"], lw["g_gdn_out"],
        batch=db, seq=t_pad, t_valid=nq, tc=t_pad, chunk=t_pad)
    o_gdn = to_tm(o_gdn.reshape(db, t_pad, gw)[:, :nq])
    nheads_m = cache_k.shape[3]
    heads = lambda a: to_bm(a).reshape(db, nq, nheads_m, HEAD_DIM)
    q_bm, k_bm, v_bm = heads(mqkv[0]), heads(mqkv[1]), heads(mqkv[2])
    o_att = _moba_sample(q_bm, k_bm, v_bm, cache_k, cache_v, layer, page_table, lw["g_attn_out"])
    o_att = to_tm(o_att.reshape(db, nq, nheads_m * HEAD_DIM)).astype(BF16)
    x = _outproj(o_gdn, o_att, lw["w_out_gdn"], lw["w_out_att"], x, gate1, tm=m, tn=1024, **kw)
    state_tm = ffn_conv.transpose(1, 0, 2).reshape(1, (FFN_CONV - 1) * db, dff)
    x, ffn_new = _ffn(x, scale2, shift2, gate2, lw["g_ffn"], lw["w_gate"], lw["w_up"], lw["w_dw"], lw["b_dw"],
                      lw["w_down"], state_tm, unit=db, tm=m, tn=512, **kw)
    ffn_new = ffn_new.reshape(FFN_CONV - 1, db, dff).transpose(1, 0, 2)
    return x, k_bm, v_bm, s_new, conv_new, ffn_new


def kernel(x_prompt, x_sample, cache_k, cache_v, state_gdn, state_gdn_conv, state_ffn_conv, page_table, c_prompt, c_sample, w_ada, b_ada, g_norm_mix, w_in, w_gdn_conv, gdn_a_log, gdn_dt_bias, g_gdn_out, g_q, g_k, g_attn_out, w_out, g_norm_ffn, w_ffn_gate, w_ffn_up, w_ffn_dw, b_ffn_dw, w_ffn_down):
    depth = w_in.shape[0]
    batch, seq, d = x_prompt.shape
    db, nq, _ = x_sample.shape
    nheads_m = cache_k.shape[3]

    xp = x_prompt.reshape(batch * seq, d)
    xs = x_sample.transpose(1, 0, 2).reshape(nq * db, d)
    c_all = jnp.concatenate([c_prompt, c_sample], axis=0)

    outs_p = [[] for _ in range(5)]
    outs_s = [[] for _ in range(5)]
    for l in range(depth):
        lw = _layer_weights(l, w_in, w_gdn_conv, gdn_a_log, gdn_dt_bias, g_gdn_out, g_q, g_k, g_attn_out, w_out,
                            g_norm_mix, g_norm_ffn, w_ffn_gate, w_ffn_up, w_ffn_dw, b_ffn_dw, w_ffn_down)
        mod = _ada(c_all, w_ada, b_ada, l).reshape(batch + db, 6, d)
        mod_s = tuple(jnp.tile(mod[batch:, i], (nq, 1)) for i in range(6))

        xp, k_p, v_p, s_p, gc_p, fc_p = _prompt_layer(xp, mod[:batch], lw, batch=batch, seq=seq)
        for acc, val in zip(outs_p, (k_p.reshape(batch, seq, nheads_m, HEAD_DIM),
                                     v_p.reshape(batch, seq, nheads_m, HEAD_DIM), s_p, gc_p, fc_p)):
            acc.append(val)

        xs, k_s, v_s, s_s, gc_s, fc_s = _sample_layer(
            xs, mod_s, lw, cache_k, cache_v, l, page_table, state_gdn, state_gdn_conv[l], state_ffn_conv[l],
            db=db, nq=nq)
        for acc, val in zip(outs_s, (k_s, v_s, s_s, gc_s, fc_s)):
            acc.append(val)

    y_prompt = xp.reshape(batch, seq, d)
    y_sample = xs.reshape(nq, db, d).transpose(1, 0, 2)
    return (y_prompt, y_sample, *(jnp.stack(a) for a in outs_p), *(jnp.stack(a) for a in outs_s))
```
